```python
import math
import jax, jax.numpy as jnp
from jax import lax
import numpy as np

D_MODEL = 1024
BATCH = 2
SEQ = 8192
DEPTH = 2

CHUNK = 64
N_A = DEPTH // 2
N_B = DEPTH - N_A
EPS = 1e-6

POOL_WINDOWS = (2, 4, 8, 16)
N_POOL = len(POOL_WINDOWS)
POOL_GD = D_MODEL // N_POOL

N_HEADS = 8
QK_NOPE = 128
QK_ROPE = 64
V_DIM = 128
Q_RANK = 384
KV_RANK = 256
ROPE_THETA = 10000.0
ATTN_SCALE = 1.0 / math.sqrt(QK_NOPE + QK_ROPE)
Q_BLOCK = 128

N_GROUPS = 4
EXPERTS_PER_GROUP = 8
N_EXPERTS = N_GROUPS * EXPERTS_PER_GROUP
TOP_K = 2
D_EXPERT = 512
ROW_BLOCK = 256

kernel_name = "yoco_pool_mla_hier_moe"


def rms_norm(x, g):
    xf = x.astype(jnp.float32)
    y = xf * lax.rsqrt(jnp.mean(xf * xf, axis=-1, keepdims=True) + EPS)
    return (y * g.astype(jnp.float32)).astype(x.dtype)


def rope_tables(seq, dtype):
    half = QK_ROPE // 2
    inv = ROPE_THETA ** (-jnp.arange(half, dtype=jnp.float32) / half)
    ang = jnp.arange(seq, dtype=jnp.float32)[:, None] * inv[None, :]
    return jnp.cos(ang).astype(dtype), jnp.sin(ang).astype(dtype)


def apply_rope(x, cos, sin):
    half = QK_ROPE // 2
    x1, x2 = x[..., :half], x[..., half:]
    return jnp.concatenate([x1 * cos - x2 * sin, x2 * cos + x1 * sin], axis=-1)


def pool_mixer(h, w, b, scale):
    B, S, D = h.shape
    hg = h.reshape(B, S, N_POOL, POOL_GD).astype(jnp.float32)
    csum = jnp.pad(jnp.cumsum(hg, axis=1), ((0, 0), (1, 0), (0, 0), (0, 0)))
    t = jnp.arange(S)
    means = []
    for g, win in enumerate(POOL_WINDOWS):
        lo = jnp.maximum(t + 1 - win, 0)
        cnt = (t + 1 - lo).astype(jnp.float32)
        means.append((csum[:, 1:, g] - csum[:, lo, g]) / cnt[:, None])
    pooled = (jnp.stack(means, axis=2) - hg).astype(h.dtype)
    y = jnp.einsum('bsgc,gcd->bsgd', pooled, w) + b
    return y.reshape(B, S, D) * scale


def shared_kv(x, kv_in_norm, w_dkv, kv_norm, w_uk, w_uv, cos, sin):
    B, S, _ = x.shape
    hs = rms_norm(x, kv_in_norm)
    ckv = hs @ w_dkv
    c_kv = rms_norm(ckv[..., :KV_RANK], kv_norm)
    k_rope = apply_rope(ckv[..., KV_RANK:], cos, sin)
    k_nope = (c_kv @ w_uk).reshape(B, S, N_HEADS, QK_NOPE)
    v = (c_kv @ w_uv).reshape(B, S, N_HEADS, V_DIM)
    return k_nope, k_rope, v


def mla_attention(h, k_nope, k_rope, v, wq_down, q_norm, wq_up, wo, cos, sin):
    B, S, _ = h.shape
    cq = rms_norm(h @ wq_down, q_norm)
    q = (cq @ wq_up).reshape(B, S, N_HEADS, QK_NOPE + QK_ROPE)
    q_nope = q[..., :QK_NOPE]
    q_rope = apply_rope(q[..., QK_NOPE:], cos[:, None, :], sin[:, None, :])
    nb = S // Q_BLOCK
    qn_b = q_nope.reshape(B, nb, Q_BLOCK, N_HEADS, QK_NOPE).transpose(1, 0, 2, 3, 4)
    qr_b = q_rope.reshape(B, nb, Q_BLOCK, N_HEADS, QK_ROPE).transpose(1, 0, 2, 3, 4)
    k_chunk = jnp.arange(S) // CHUNK

    def attend(args):
        qn, qr, i = args
        s = (jnp.einsum('bqhd,bkhd->bhqk', qn, k_nope)
             + jnp.einsum('bqhr,bkr->bhqk', qr, k_rope)).astype(jnp.float32) * ATTN_SCALE
        q_chunk = (i * Q_BLOCK + jnp.arange(Q_BLOCK)) // CHUNK
        mask = k_chunk[None, :] <= q_chunk[:, None]
        s = jnp.where(mask[None, None], s, -jnp.inf)
        p = jax.nn.softmax(s, axis=-1).astype(v.dtype)
        return jnp.einsum('bhqk,bkhd->bqhd', p, v)

    o = lax.map(attend, (qn_b, qr_b, jnp.arange(nb)))
    o = o.transpose(1, 0, 2, 3, 4).reshape(B, S, N_HEADS * V_DIM)
    return o @ wo


def hier_moe(h, rg_w, rg_b, re_w, re_b, w_gate, w_up, w_down):
    B, S, D = h.shape
    T = B * S
    ht = h.reshape(T, D)
    gp = jax.nn.softmax((ht @ rg_w + rg_b).astype(jnp.float32), axis=-1)
    g_idx = jnp.argmax(gp, axis=-1)
    g_prob = jnp.take_along_axis(gp, g_idx[:, None], axis=-1)[:, 0]
    el = (ht @ re_w + re_b).astype(jnp.float32).reshape(T, N_GROUPS, EXPERTS_PER_GROUP)
    el = jnp.take_along_axis(el, g_idx[:, None, None], axis=1)[:, 0]
    top_p, top_i = lax.top_k(jax.nn.softmax(el, axis=-1), TOP_K)
    gates = g_prob[:, None] * top_p / jnp.sum(top_p, axis=-1, keepdims=True)
    expert_ids = (g_idx[:, None] * EXPERTS_PER_GROUP + top_i).astype(jnp.int32)

    N = T * TOP_K
    flat_e = expert_ids.reshape(N)
    flat_tok = jnp.arange(N, dtype=jnp.int32) // TOP_K
    flat_g = gates.reshape(N)
    order = jnp.argsort(flat_e, stable=True)
    sorted_e = flat_e[order]
    counts = jnp.bincount(flat_e, length=N_EXPERTS).astype(jnp.int32)
    padded = ((counts + ROW_BLOCK - 1) // ROW_BLOCK) * ROW_BLOCK
    pad_end = jnp.cumsum(padded)
    pad_start = pad_end - padded
    start = jnp.cumsum(counts) - counts
    dest = pad_start[sorted_e] + (jnp.arange(N, dtype=jnp.int32) - start[sorted_e])
    n_blocks = N // ROW_BLOCK + N_EXPERTS
    n_rows = n_blocks * ROW_BLOCK
    row_tok = jnp.zeros((n_rows,), jnp.int32).at[dest].set(flat_tok[order])
    row_w = jnp.zeros((n_rows,), jnp.float32).at[dest].set(flat_g[order])
    block_e = jnp.clip(jnp.searchsorted(pad_end, jnp.arange(n_blocks) * ROW_BLOCK, side='right'),
                       0, N_EXPERTS - 1).astype(jnp.int32)
    xb = ht[row_tok].reshape(n_blocks, ROW_BLOCK, D)

    def expert_block(args):
        xr, e = args
        return (jax.nn.silu(xr @ w_gate[e]) * (xr @ w_up[e])) @ w_down[e]

    y = lax.map(expert_block, (xb, block_e)).reshape(n_rows, D).astype(jnp.float32)
    out = jnp.zeros((T, D), jnp.float32).at[row_tok].add(y * row_w[:, None])
    return out.astype(h.dtype).reshape(B, S, D)


def setup_inputs(seed: int = 0) -> dict:
    key = jax.random.key(seed)
    ks = iter(jax.random.split(key, 32))
    f32 = jnp.float32

    def nrm(shape, fan_in):
        return jax.random.normal(next(ks), shape, f32) * (fan_in ** -0.5)

    def gain(shape):
        return 1.0 + 0.02 * jax.random.normal(next(ks), shape, f32)

    def small(shape, s=0.01):
        return s * jax.random.normal(next(ks), shape, f32)

    return {
        "x": jax.random.normal(next(ks), (BATCH, SEQ, D_MODEL), f32),
        "pool_norm": gain((N_A, D_MODEL)),
        "pool_w": nrm((N_A, N_POOL, POOL_GD, POOL_GD), POOL_GD),
        "pool_b": small((N_A, N_POOL, POOL_GD)),
        "pool_scale": gain((N_A, D_MODEL)),
        "kv_in_norm": gain((D_MODEL,)),
        "w_dkv": nrm((D_MODEL, KV_RANK + QK_ROPE), D_MODEL),
        "kv_norm": gain((KV_RANK,)),
        "w_uk": nrm((KV_RANK, N_HEADS * QK_NOPE), KV_RANK),
        "w_uv": nrm((KV_RANK, N_HEADS * V_DIM), KV_RANK),
        "attn_norm": gain((N_B, D_MODEL)),
        "wq_down": nrm((N_B, D_MODEL, Q_RANK), D_MODEL),
        "q_norm": gain((N_B, Q_RANK)),
        "wq_up": nrm((N_B, Q_RANK, N_HEADS * (QK_NOPE + QK_ROPE)), Q_RANK),
        "wo": nrm((N_B, N_HEADS * V_DIM, D_MODEL), N_HEADS * V_DIM),
        "ffn_norm": gain((DEPTH, D_MODEL)),
        "router_group_w": nrm((DEPTH, D_MODEL, N_GROUPS), D_MODEL),
        "router_group_b": small((DEPTH, N_GROUPS)),
        "router_expert_w": nrm((DEPTH, D_MODEL, N_EXPERTS), D_MODEL),
        "router_expert_b": small((DEPTH, N_EXPERTS)),
        "w_gate": nrm((DEPTH, N_EXPERTS, D_MODEL, D_EXPERT), D_MODEL),
        "w_up": nrm((DEPTH, N_EXPERTS, D_MODEL, D_EXPERT), D_MODEL),
        "w_down": nrm((DEPTH, N_EXPERTS, D_EXPERT, D_MODEL), D_EXPERT),
        "final_norm": gain((D_MODEL,)),
    }


def reference(x, pool_norm, pool_w, pool_b, pool_scale, kv_in_norm, w_dkv, kv_norm, w_uk, w_uv,
              attn_norm, wq_down, q_norm, wq_up, wo, ffn_norm, router_group_w, router_group_b,
              router_expert_w, router_expert_b, w_gate, w_up, w_down, final_norm):
    S = x.shape[1]
    cos, sin = rope_tables(S, x.dtype)
    k_nope = k_rope = v = None
    for l in range(DEPTH):
        if l < N_A:
            x = x + pool_mixer(rms_norm(x, pool_norm[l]), pool_w[l], pool_b[l], pool_scale[l])
        else:
            if l == N_A:
                k_nope, k_rope, v = shared_kv(x, kv_in_norm, w_dkv, kv_norm, w_uk, w_uv, cos, sin)
            j = l - N_A
            x = x + mla_attention(rms_norm(x, attn_norm[j]), k_nope, k_rope, v,
                                  wq_down[j], q_norm[j], wq_up[j], wo[j], cos, sin)
        x = x + hier_moe(rms_norm(x, ffn_norm[l]), router_group_w[l], router_group_b[l],
                         router_expert_w[l], router_expert_b[l], w_gate[l], w_up[l], w_down[l])
    return rms_norm(x, final_norm)
```

```python
import functools
import math

import jax
import jax.numpy as jnp
from jax import lax
from jax.experimental import pallas as pl
from jax.experimental.pallas import tpu as pltpu

EPS = 1e-6
CHUNK = 64
POOL_WINDOWS = (2, 4, 8, 16)
N_HEADS = 8
QK_NOPE = 128
QK_ROPE = 64
V_DIM = 128
KV_RANK = 256
ROPE_THETA = 10000.0
ATTN_SCALE = 1.0 / math.sqrt(QK_NOPE + QK_ROPE)
N_GROUPS = 4
EXPERTS_PER_GROUP = 8
N_EXPERTS = N_GROUPS * EXPERTS_PER_GROUP
ROW_BLOCK = 256

LANES = 128
HALO = 16
QK_PAD = 256
VMEM_LIMIT = 56 * 1024 * 1024
NEG_BIG = -1e30

F32 = jnp.float32
BF16 = jnp.bfloat16


def _cparams(*sem):
    return pltpu.CompilerParams(dimension_semantics=sem, vmem_limit_bytes=VMEM_LIMIT)


def _rms(x, g):
    return x * lax.rsqrt(jnp.mean(x * x, axis=-1, keepdims=True) + EPS) * g


def _pack_bf16_pairs(h):
    m = h.shape[1] // 2
    bits = pltpu.bitcast(h.astype(BF16).astype(F32), jnp.uint32)
    return (bits[:, :m] >> 16) | bits[:, m:]


def _unpack_bf16_pairs(w):
    lo = pltpu.bitcast(w << 16, F32).astype(BF16)
    hi = pltpu.bitcast(w & jnp.uint32(0xFFFF0000), F32).astype(BF16)
    return lo, hi


def _ffn_prologue(x, fn, wr, br, hp_ref, lg_ref):
    h = _rms(x, fn)
    hp_ref[...] = _pack_bf16_pairs(h)
    lg_ref[...] = jnp.dot(h, wr, precision=lax.Precision.HIGHEST,
                          preferred_element_type=F32) + br


def _pool_kernel(x_ref, pn_ref, pw_ref, pb_ref, ps_ref, fn_ref, wr_ref, br_ref,
                 x1_ref, hp_ref, lg_ref, buf_ref, *, ts, tiles_per_seq):
    i = pl.program_id(0)
    seq_tile = i % tiles_per_seq
    x = x_ref[...]
    h = _rms(x, pn_ref[...])

    @pl.when(seq_tile == 0)
    def _():
        buf_ref[0:HALO, :] = jnp.zeros((HALO, x.shape[1]), F32)

    buf_ref[HALO:, :] = h
    pos = lax.broadcasted_iota(jnp.int32, (ts, 1), 0) + seq_tile * ts
    gd = x.shape[1] // len(POOL_WINDOWS)
    for g, win in enumerate(POOL_WINDOWS):
        cols = slice(g * gd, (g + 1) * gd)
        s = buf_ref[:, cols]
        k = 1
        while k < win:
            s = s + pltpu.roll(s, k, axis=0)
            k *= 2
        cnt = jnp.minimum(pos + 1, win).astype(F32)
        pooled = s[HALO:, :] / cnt - h[:, cols]
        y = jnp.dot(pooled.astype(BF16), pw_ref[g], preferred_element_type=F32) + pb_ref[g]
        x1_ref[:, cols] = x[:, cols] + y * ps_ref[:, cols]
    buf_ref[0:HALO, :] = h[ts - HALO:, :]
    _ffn_prologue(x1_ref[...], fn_ref[...], wr_ref[...], br_ref[...], hp_ref, lg_ref)


def _pool_layer(x2d, seq, pn, pw, pb, ps, fn, wr, br, ts):
    t, d = x2d.shape
    full = lambda *shape: pl.BlockSpec(shape, lambda i: (0,) * len(shape))
    return pl.pallas_call(
        functools.partial(_pool_kernel, ts=ts, tiles_per_seq=seq // ts),
        grid=(t // ts,),
        in_specs=[pl.BlockSpec((ts, d), lambda i: (i, 0)),
                  full(1, d), full(*pw.shape), full(*pb.shape), full(1, d), full(1, d),
                  full(d, LANES), full(1, LANES)],
        out_specs=[pl.BlockSpec((ts, d), lambda i: (i, 0)),
                   pl.BlockSpec((ts, d // 2), lambda i: (i, 0)),
                   pl.BlockSpec((ts, LANES), lambda i: (i, 0))],
        out_shape=[jax.ShapeDtypeStruct((t, d), F32),
                   jax.ShapeDtypeStruct((t, d // 2), jnp.uint32),
                   jax.ShapeDtypeStruct((t, LANES), F32)],
        scratch_shapes=[pltpu.VMEM((HALO + ts, d), F32)],
        compiler_params=_cparams("arbitrary"),
        name="pool_layer",
    )(x2d, pn, pw, pb, ps, fn, wr, br)


def _route_kernel(lg_ref, meta_ref, cnt_ref, carry_ref, *, tr):
    i = pl.program_id(0)

    @pl.when(i == 0)
    def _():
        carry_ref[...] = jnp.zeros_like(carry_ref)

    lg = lg_ref[...]
    col = lax.broadcasted_iota(jnp.int32, lg.shape, 1)
    neg = jnp.float32(-jnp.inf)
    first = lambda hit: jnp.min(jnp.where(hit, col, LANES), axis=-1, keepdims=True)

    gl = jnp.where(col < N_GROUPS, lg, neg)
    gmax = jnp.max(gl, axis=-1, keepdims=True)
    gidx = first(gl == gmax)
    gprob = 1.0 / jnp.sum(jnp.exp(gl - gmax), axis=-1, keepdims=True)

    lo = N_GROUPS + EXPERTS_PER_GROUP * gidx
    el = jnp.where(col >= lo, jnp.where(col < lo + EXPERTS_PER_GROUP, lg, neg), neg)
    m1 = jnp.max(el, axis=-1, keepdims=True)
    i1 = first(el == m1)
    el2 = jnp.where(col == i1, neg, el)
    m2 = jnp.max(el2, axis=-1, keepdims=True)
    i2 = first(el2 == m2)
    r = jnp.exp(m2 - m1)
    g0 = gprob / (1.0 + r)
    g1 = gprob * r / (1.0 + r)
    e0 = i1 - N_GROUPS
    e1 = i2 - N_GROUPS

    oh0 = jnp.where(col == e0, 1.0, 0.0)
    oh1 = jnp.where(col == e1, 1.0, 0.0)
    oh = oh0 + oh1
    rr = lax.broadcasted_iota(jnp.int32, (tr, tr), 0)
    cc = lax.broadcasted_iota(jnp.int32, (tr, tr), 1)
    tri = jnp.where(rr > cc, 1.0, 0.0).astype(BF16)
    before = jnp.dot(tri, oh.astype(BF16), preferred_element_type=F32) + carry_ref[0:1, :]
    rank0 = jnp.sum(before * oh0, axis=-1, keepdims=True)
    rank1 = jnp.sum(before * oh1, axis=-1, keepdims=True)
    carry_ref[...] = carry_ref[...] + jnp.sum(oh, axis=0, keepdims=True)
    cnt_ref[...] = carry_ref[...]

    vals = (e0.astype(F32), e1.astype(F32), g0, g1, rank0, rank1)
    meta = jnp.zeros(lg.shape, F32)
    for c, v in enumerate(vals):
        meta = jnp.where(col == c, v, meta)
    meta_ref[...] = meta


def _route(logits, tr):
    t = logits.shape[0]
    return pl.pallas_call(
        functools.partial(_route_kernel, tr=tr),
        grid=(t // tr,),
        in_specs=[pl.BlockSpec((tr, LANES), lambda i: (i, 0))],
        out_specs=[pl.BlockSpec((tr, LANES), lambda i: (i, 0)),
                   pl.BlockSpec((8, LANES), lambda i: (0, 0))],
        out_shape=[jax.ShapeDtypeStruct((t, LANES), F32),
                   jax.ShapeDtypeStruct((8, LANES), F32)],
        scratch_shapes=[pltpu.VMEM((8, LANES), F32)],
        compiler_params=_cparams("arbitrary"),
        name="route",
    )(logits)


def _dispatch_plan(meta, cnt, n_blocks):
    e = meta[:, 0:2].astype(jnp.int32)
    rank = meta[:, 4:6].astype(jnp.int32)
    counts = cnt[0, :N_EXPERTS].astype(jnp.int32)
    padded = ((counts + ROW_BLOCK - 1) // ROW_BLOCK) * ROW_BLOCK
    pad_end = jnp.cumsum(padded)
    pad_start = pad_end - padded
    dest = pad_start[e] + rank
    nb_used = (pad_end[-1:] // ROW_BLOCK).astype(jnp.int32)
    block_e = jnp.clip(jnp.searchsorted(pad_end, jnp.arange(n_blocks, dtype=jnp.int32) * ROW_BLOCK,
                                        side='right'), 0, N_EXPERTS - 1).astype(jnp.int32)
    return dest, block_e, nb_used


def _dispatch_kernel(dest_ref, h_ref, xs_in_ref, xs_ref, sem, *, ts):
    del xs_in_ref

    def issue(r, c):
        for k in range(2):
            d = dest_ref[0, 0, 2 * r + k]
            pltpu.make_async_copy(h_ref.at[pl.ds(r, 1)], xs_ref.at[pl.ds(d, 1)], sem).start()
        return c

    lax.fori_loop(0, ts, issue, 0)
    for k in range(2):
        pltpu.make_async_copy(h_ref, xs_ref.at[pl.ds(0, ts)], sem).wait()


def _dispatch(hp, dest, n_rows, ts):
    t, dh = hp.shape
    dest3 = dest.reshape(t // ts, 1, 2 * ts)
    xs0 = jnp.zeros((n_rows, dh), jnp.uint32)
    return pl.pallas_call(
        functools.partial(_dispatch_kernel, ts=ts),
        grid=(t // ts,),
        in_specs=[pl.BlockSpec((1, 1, 2 * ts), lambda i: (i, 0, 0), memory_space=pltpu.SMEM),
                  pl.BlockSpec((ts, dh), lambda i: (i, 0)),
                  pl.BlockSpec(memory_space=pl.ANY)],
        out_specs=pl.BlockSpec(memory_space=pl.ANY),
        out_shape=jax.ShapeDtypeStruct((n_rows, dh), jnp.uint32),
        scratch_shapes=[pltpu.SemaphoreType.DMA(())],
        input_output_aliases={2: 0},
        compiler_params=_cparams("arbitrary"),
        name="dispatch",
    )(dest3, hp, xs0)


def _expert_kernel(be_ref, nu_ref, xs_ref, wg_ref, wu_ref, wd_ref, ys_ref, wg_s, wu_s, wd_s):
    b = pl.program_id(0)

    @pl.when(b < nu_ref[0])
    def _():
        prev = be_ref[jnp.maximum(b - 1, 0)]

        @pl.when((b == 0) | (be_ref[b] != prev))
        def _():
            wg_s[...] = wg_ref[0].astype(BF16)
            wu_s[...] = wu_ref[0].astype(BF16)
            wd_s[...] = wd_ref[0].astype(BF16)

        xa, xb = _unpack_bf16_pairs(xs_ref[...])
        half = xa.shape[1]
        mm = lambda a, w: jnp.dot(a, w, preferred_element_type=F32)
        gt = mm(xa, wg_s[0:half, :]) + mm(xb, wg_s[half:, :])
        up = mm(xa, wu_s[0:half, :]) + mm(xb, wu_s[half:, :])
        act = gt * (1.0 / (1.0 + jnp.exp(-gt))) * up
        ys_ref[...] = mm(act.astype(BF16), wd_s[...])

    @pl.when(b >= nu_ref[0])
    def _():
        ys_ref[...] = jnp.zeros_like(ys_ref)


def _experts(xs, block_e, nb_used, w_gate, w_up, w_down):
    n_rows, dh = xs.shape
    n_blocks = n_rows // ROW_BLOCK
    _, d, de = w_gate.shape
    blk = lambda b, be, nu: (jnp.minimum(b, nu[0] - 1), 0)
    wsel = lambda b, be, nu: (be[jnp.minimum(b, nu[0] - 1)], 0, 0)
    return pl.pallas_call(
        _expert_kernel,
        grid_spec=pltpu.PrefetchScalarGridSpec(
            num_scalar_prefetch=2,
            grid=(n_blocks,),
            in_specs=[pl.BlockSpec((ROW_BLOCK, dh), blk),
                      pl.BlockSpec((1, d, de), wsel),
                      pl.BlockSpec((1, d, de), wsel),
                      pl.BlockSpec((1, de, d), wsel)],
            out_specs=pl.BlockSpec((ROW_BLOCK, d), lambda b, be, nu: (b, 0)),
            scratch_shapes=[pltpu.VMEM((d, de), BF16), pltpu.VMEM((d, de), BF16),
                            pltpu.VMEM((de, d), BF16)]),
        out_shape=jax.ShapeDtypeStruct((n_rows, d), F32),
        compiler_params=_cparams("arbitrary"),
        name="experts",
    )(block_e, nb_used, xs, w_gate, w_up, w_down)


def _combine_kernel(dest_ref, x_ref, meta_ref, ys_ref, fin_ref, out_ref, ybuf, sem, *, ts, final):
    def issue(r, c):
        for k in range(2):
            d = dest_ref[0, 0, 2 * r + k]
            pltpu.make_async_copy(ys_ref.at[pl.ds(d, 1)], ybuf.at[k, pl.ds(r, 1)], sem).start()
        return c

    lax.fori_loop(0, ts, issue, 0)
    for k in range(2):
        pltpu.make_async_copy(ys_ref.at[pl.ds(0, ts)], ybuf.at[k], sem).wait()
    meta = meta_ref[...]
    out = x_ref[...] + meta[:, 2:3] * ybuf[0] + meta[:, 3:4] * ybuf[1]
    if final:
        out = _rms(out, fin_ref[...])
    out_ref[...] = out


def _combine(x2d, meta, dest, ys, fin, ts, final):
    t, d = x2d.shape
    dest3 = dest.reshape(t // ts, 1, 2 * ts)
    return pl.pallas_call(
        functools.partial(_combine_kernel, ts=ts, final=final),
        grid=(t // ts,),
        in_specs=[pl.BlockSpec((1, 1, 2 * ts), lambda i: (i, 0, 0), memory_space=pltpu.SMEM),
                  pl.BlockSpec((ts, d), lambda i: (i, 0)),
                  pl.BlockSpec((ts, LANES), lambda i: (i, 0)),
                  pl.BlockSpec(memory_space=pl.ANY),
                  pl.BlockSpec((1, d), lambda i: (0, 0))],
        out_specs=pl.BlockSpec((ts, d), lambda i: (i, 0)),
        out_shape=jax.ShapeDtypeStruct((t, d), F32),
        scratch_shapes=[pltpu.VMEM((2, ts, d), F32), pltpu.SemaphoreType.DMA(())],
        compiler_params=_cparams("arbitrary"),
        name="combine",
    )(dest3, x2d, meta, ys, fin)


def _moe(x2d, hp, logits, w_gate, w_up, w_down, fin, final, ts):
    t = x2d.shape[0]
    n_blocks = (2 * t) // ROW_BLOCK + N_EXPERTS
    meta, cnt = _route(logits, min(512, t))
    dest, block_e, nb_used = _dispatch_plan(meta, cnt, n_blocks)
    xs = _dispatch(hp, dest, n_blocks * ROW_BLOCK, ts)
    ys = _experts(xs, block_e, nb_used, w_gate, w_up, w_down)
    return _combine(x2d, meta, dest, ys, fin, ts, final)


def _rope_pairs(t, cs):
    p = t * cs
    return p + pltpu.roll(p, QK_ROPE, axis=1)


def _qkv_kernel(x_ref, kn_ref, an_ref, wdkv_ref, kvn_ref, wuk_ref, wuv_ref, wqd_ref, qn_ref, wqu_ref,
                cs_ref, k_ref, v_ref, q_ref):
    x = x_ref[...]
    xn = x * lax.rsqrt(jnp.mean(x * x, axis=-1, keepdims=True) + EPS)
    cs = cs_ref[...]
    mm = lambda a, w: jnp.dot(a.astype(BF16), w, preferred_element_type=F32)
    lane = lax.broadcasted_iota(jnp.int32, cs.shape, 1)

    ckv = mm(xn * kn_ref[...], wdkv_ref[...])
    c_kv = _rms(ckv[:, :KV_RANK], kvn_ref[...])
    k_rope = jnp.where(lane < QK_ROPE, _rope_pairs(ckv[:, KV_RANK:], cs), 0.0).astype(BF16)
    k_nope = mm(c_kv, wuk_ref[...]).astype(BF16)
    v_ref[0] = mm(c_kv, wuv_ref[...]).astype(BF16)

    cq = _rms(mm(xn * an_ref[...], wqd_ref[...]), qn_ref[...])
    q = mm(cq, wqu_ref[...]) * ATTN_SCALE
    nope_w = N_HEADS * QK_NOPE
    for h in range(N_HEADS):
        hs = slice(h * QK_NOPE, (h + 1) * QK_NOPE)
        k_ref[0, h] = jnp.concatenate([k_nope[:, hs], k_rope], axis=-1)
        rs = slice(nope_w + h * LANES, nope_w + (h + 1) * LANES)
        q_ref[0, h] = jnp.concatenate([q[:, hs], _rope_pairs(q[:, rs], cs)], axis=-1).astype(BF16)


def _qkv(x2d, batch, seq, kn, an, wdkv, kvn, wuk, wuv, wqd, qn, wqu, cs, ts):
    t, d = x2d.shape
    tps = seq // ts
    full = lambda a: pl.BlockSpec(a.shape, lambda i: (0,) * a.ndim)
    hspec = pl.BlockSpec((1, N_HEADS, ts, QK_PAD), lambda i: (i // tps, 0, i % tps, 0))
    return pl.pallas_call(
        _qkv_kernel,
        grid=(t // ts,),
        in_specs=[pl.BlockSpec((ts, d), lambda i: (i, 0)),
                  full(kn), full(an), full(wdkv), full(kvn), full(wuk), full(wuv), full(wqd), full(qn),
                  full(wqu), pl.BlockSpec((ts, LANES), lambda i: (i % tps, 0))],
        out_specs=[hspec,
                   pl.BlockSpec((1, ts, N_HEADS * V_DIM), lambda i: (i // tps, i % tps, 0)),
                   hspec],
        out_shape=[jax.ShapeDtypeStruct((batch, N_HEADS, seq, QK_PAD), BF16),
                   jax.ShapeDtypeStruct((batch, seq, N_HEADS * V_DIM), BF16),
                   jax.ShapeDtypeStruct((batch, N_HEADS, seq, QK_PAD), BF16)],
        compiler_params=_cparams("arbitrary"),
        name="qkv_proj",
    )(x2d, kn, an, wdkv, kvn, wuk, wuv, wqd, qn, wqu, cs)


def _attn_kernel(q_ref, k_ref, v_ref, o_ref, *, tq):
    i = pl.program_id(2)
    q = q_ref[0, 0]

    def tile(j, carry, masked):
        m, l, acc = carry
        start = pl.multiple_of(j * tq, tq)
        k = k_ref[0, 0, pl.ds(start, tq), :]
        v = v_ref[0, pl.ds(start, tq), :]
        s = lax.dot_general(q, k, (((1,), (1,)), ((), ())), preferred_element_type=F32)
        if masked:
            rq = lax.broadcasted_iota(jnp.int32, s.shape, 0) // CHUNK
            ck = lax.broadcasted_iota(jnp.int32, s.shape, 1) // CHUNK
            s = jnp.where(ck <= rq, s, NEG_BIG)
        m_new = jnp.maximum(m, jnp.max(s, axis=-1, keepdims=True))
        p = jnp.exp(s - m_new)
        alpha = jnp.exp(m - m_new)
        l = alpha * l + jnp.sum(p, axis=-1, keepdims=True)
        acc = alpha * acc + jnp.dot(p.astype(BF16), v, preferred_element_type=F32)
        return m_new, l, acc

    init = (jnp.full((tq, 1), NEG_BIG, F32), jnp.zeros((tq, 1), F32), jnp.zeros((tq, V_DIM), F32))
    carry = lax.fori_loop(0, i, lambda j, c: tile(j, c, False), init)
    _, l, acc = tile(i, carry, True)
    o_ref[0] = (acc / l).astype(BF16)


def _attention(qc, kc, v, tq):
    batch, _, seq, _ = qc.shape
    return pl.pallas_call(
        functools.partial(_attn_kernel, tq=tq),
        grid=(batch, N_HEADS, seq // tq),
        in_specs=[pl.BlockSpec((1, 1, tq, QK_PAD), lambda b, h, i: (b, h, i, 0)),
                  pl.BlockSpec((1, 1, seq, QK_PAD), lambda b, h, i: (b, h, 0, 0)),
                  pl.BlockSpec((1, seq, V_DIM), lambda b, h, i: (b, 0, h))],
        out_specs=pl.BlockSpec((1, tq, V_DIM), lambda b, h, i: (b, i, h)),
        out_shape=jax.ShapeDtypeStruct((batch, seq, N_HEADS * V_DIM), BF16),
        compiler_params=_cparams("arbitrary", "arbitrary", "arbitrary"),
        name="attention",
    )(qc, kc, v)


def _oproj_kernel(o_ref, x_ref, wo_ref, fn_ref, wr_ref, br_ref, x3_ref, hp_ref, lg_ref):
    x3 = x_ref[...] + jnp.dot(o_ref[...], wo_ref[...], preferred_element_type=F32)
    x3_ref[...] = x3
    _ffn_prologue(x3, fn_ref[...], wr_ref[...], br_ref[...], hp_ref, lg_ref)


def _oproj(o2d, x2d, wo, fn, wr, br, ts):
    t, d = x2d.shape
    full = lambda a: pl.BlockSpec(a.shape, lambda i: (0,) * a.ndim)
    return pl.pallas_call(
        _oproj_kernel,
        grid=(t // ts,),
        in_specs=[pl.BlockSpec((ts, o2d.shape[1]), lambda i: (i, 0)),
                  pl.BlockSpec((ts, d), lambda i: (i, 0)),
                  full(wo), full(fn), full(wr), full(br)],
        out_specs=[pl.BlockSpec((ts, d), lambda i: (i, 0)),
                   pl.BlockSpec((ts, d // 2), lambda i: (i, 0)),
                   pl.BlockSpec((ts, LANES), lambda i: (i, 0))],
        out_shape=[jax.ShapeDtypeStruct((t, d), F32),
                   jax.ShapeDtypeStruct((t, d // 2), jnp.uint32),
                   jax.ShapeDtypeStruct((t, LANES), F32)],
        compiler_params=_cparams("arbitrary"),
        name="out_proj",
    )(o2d, x2d, wo, fn, wr, br)


def _router_params(rg_w, rg_b, re_w, re_b):
    d = rg_w.shape[0]
    used = N_GROUPS + N_EXPERTS
    wr = jnp.concatenate([rg_w, re_w, jnp.zeros((d, LANES - used), F32)], axis=1)
    br = jnp.concatenate([rg_b, re_b, jnp.zeros((LANES - used,), F32)])[None, :]
    return wr, br


def _with_rotate_half(w):
    half = w.shape[-1] // 2
    return jnp.concatenate([w, -w[..., half:], w[..., :half]], axis=-1)


def _rope_table(seq):
    half = QK_ROPE // 2
    inv = ROPE_THETA ** (-jnp.arange(half, dtype=F32) / half)
    ang = jnp.arange(seq, dtype=F32)[:, None] * inv[None, :]
    cos, sin = jnp.cos(ang), jnp.sin(ang)
    return jnp.concatenate([cos, cos, sin, sin], axis=1)


def kernel(x, pool_norm, pool_w, pool_b, pool_scale, kv_in_norm, w_dkv, kv_norm, w_uk, w_uv, attn_norm, wq_down, q_norm, wq_up, wo, ffn_norm, router_group_w, router_group_b, router_expert_w, router_expert_b, w_gate, w_up, w_down, final_norm):
    batch, seq, d = x.shape
    t = batch * seq
    depth = ffn_norm.shape[0]
    n_a = pool_norm.shape[0]
    ts = min(256, seq)
    tq = min(512, seq)
    row = lambda a: a.reshape(1, -1)

    cs = _rope_table(seq)
    routers = [_router_params(router_group_w[l], router_group_b[l], router_expert_w[l], router_expert_b[l])
               for l in range(depth)]
    wdkv = jnp.concatenate([w_dkv[:, :KV_RANK], _with_rotate_half(w_dkv[:, KV_RANK:])], axis=1).astype(BF16)
    wuk, wuv = w_uk.astype(BF16), w_uv.astype(BF16)

    x2d = x.reshape(t, d)
    kc = vv = None
    for l in range(depth):
        wr, br = routers[l]
        if l < n_a:
            x2d, hp, logits = _pool_layer(x2d, seq, row(pool_norm[l]), pool_w[l].astype(BF16), pool_b[l][:, None, :],
                                          row(pool_scale[l]), row(ffn_norm[l]), wr, br, ts)
        else:
            j = l - n_a
            wqu = wq_up[j].reshape(-1, N_HEADS, QK_NOPE + QK_ROPE)
            wqu = jnp.concatenate([wqu[:, :, :QK_NOPE].reshape(-1, N_HEADS * QK_NOPE),
                                   _with_rotate_half(wqu[:, :, QK_NOPE:]).reshape(-1, N_HEADS * LANES)],
                                  axis=1).astype(BF16)
            k_new, v_new, qc = _qkv(x2d, batch, seq, row(kv_in_norm), row(attn_norm[j]), wdkv, row(kv_norm),
                                    wuk, wuv, wq_down[j].astype(BF16), row(q_norm[j]), wqu, cs, ts)
            if kc is None:
                kc, vv = k_new, v_new
            o = _attention(qc, kc, vv, tq)
            x2d, hp, logits = _oproj(o.reshape(t, -1), x2d, wo[j].astype(BF16), row(ffn_norm[l]), wr, br, ts)
        x2d = _moe(x2d, hp, logits, w_gate[l], w_up[l], w_down[l], row(final_norm), l == depth - 1, ts)
    return x2d.reshape(batch, seq, d)
```

```python
import functools
import math

import jax
import jax.numpy as jnp
from jax import lax
from jax.experimental import pallas as pl
from jax.experimental.pallas import tpu as pltpu

EPS = 1e-6
CHUNK = 64
POOL_WINDOWS = (2, 4, 8, 16)
N_HEADS = 8
QK_NOPE = 128
QK_ROPE = 64
V_DIM = 128
KV_RANK = 256
ROPE_THETA = 10000.0
ATTN_SCALE = 1.0 / math.sqrt(QK_NOPE + QK_ROPE)
N_GROUPS = 4
EXPERTS_PER_GROUP = 8
N_EXPERTS = N_GROUPS * EXPERTS_PER_GROUP
ROW_BLOCK = 256

LANES = 128
HALO = 16
QK_PAD = 256
VMEM_LIMIT = 56 * 1024 * 1024
NEG_BIG = -1e30

F32 = jnp.float32
BF16 = jnp.bfloat16


def _cparams(*sem):
    return pltpu.CompilerParams(dimension_semantics=sem, vmem_limit_bytes=VMEM_LIMIT)


def _rms(x, g):
    return x * lax.rsqrt(jnp.mean(x * x, axis=-1, keepdims=True) + EPS) * g


def _pack_bf16_pairs(h):
    m = h.shape[1] // 2
    bits = pltpu.bitcast(h.astype(BF16).astype(F32), jnp.uint32)
    return (bits[:, :m] >> 16) | bits[:, m:]


def _unpack_bf16_pairs(w):
    lo = pltpu.bitcast(w << 16, F32).astype(BF16)
    hi = pltpu.bitcast(w & jnp.uint32(0xFFFF0000), F32).astype(BF16)
    return lo, hi


def _ffn_prologue(x, fn, wr, br, hp_ref, lg_ref):
    h = _rms(x, fn)
    hp_ref[...] = _pack_bf16_pairs(h)
    lg_ref[...] = jnp.dot(h, wr, precision=lax.Precision.HIGHEST,
                          preferred_element_type=F32) + br


def _pool_kernel(x_ref, pn_ref, pw_ref, pb_ref, ps_ref, fn_ref, wr_ref, br_ref,
                 x1_ref, hp_ref, lg_ref, buf_ref, *, ts, tiles_per_seq):
    i = pl.program_id(0)
    seq_tile = i % tiles_per_seq
    x = x_ref[...]
    h = _rms(x, pn_ref[...])

    @pl.when(seq_tile == 0)
    def _():
        buf_ref[0:HALO, :] = jnp.zeros((HALO, x.shape[1]), F32)

    buf_ref[HALO:, :] = h
    pos = lax.broadcasted_iota(jnp.int32, (ts, 1), 0) + seq_tile * ts
    gd = x.shape[1] // len(POOL_WINDOWS)
    for g, win in enumerate(POOL_WINDOWS):
        cols = slice(g * gd, (g + 1) * gd)
        s = buf_ref[:, cols]
        k = 1
        while k < win:
            s = s + pltpu.roll(s, k, axis=0)
            k *= 2
        cnt = jnp.minimum(pos + 1, win).astype(F32)
        pooled = s[HALO:, :] / cnt - h[:, cols]
        y = jnp.dot(pooled.astype(BF16), pw_ref[g], preferred_element_type=F32) + pb_ref[g]
        x1_ref[:, cols] = x[:, cols] + y * ps_ref[:, cols]
    buf_ref[0:HALO, :] = h[ts - HALO:, :]
    _ffn_prologue(x1_ref[...], fn_ref[...], wr_ref[...], br_ref[...], hp_ref, lg_ref)


def _pool_layer(x2d, seq, pn, pw, pb, ps, fn, wr, br, ts):
    t, d = x2d.shape
    full = lambda *shape: pl.BlockSpec(shape, lambda i: (0,) * len(shape))
    return pl.pallas_call(
        functools.partial(_pool_kernel, ts=ts, tiles_per_seq=seq // ts),
        grid=(t // ts,),
        in_specs=[pl.BlockSpec((ts, d), lambda i: (i, 0)),
                  full(1, d), full(*pw.shape), full(*pb.shape), full(1, d), full(1, d),
                  full(d, LANES), full(1, LANES)],
        out_specs=[pl.BlockSpec((ts, d), lambda i: (i, 0)),
                   pl.BlockSpec((ts, d // 2), lambda i: (i, 0)),
                   pl.BlockSpec((ts, LANES), lambda i: (i, 0))],
        out_shape=[jax.ShapeDtypeStruct((t, d), F32),
                   jax.ShapeDtypeStruct((t, d // 2), jnp.uint32),
                   jax.ShapeDtypeStruct((t, LANES), F32)],
        scratch_shapes=[pltpu.VMEM((HALO + ts, d), F32)],
        compiler_params=_cparams("arbitrary"),
        name="pool_layer",
    )(x2d, pn, pw, pb, ps, fn, wr, br)


def _rows8(vals, width):
    row8 = lax.broadcasted_iota(jnp.int32, (8, width), 0)
    out = jnp.zeros((8, width), F32)
    for c, v in enumerate(vals):
        out = jnp.where(row8 == c, v, out)
    return out


def _route_kernel(lg_ref, gate_ref, plan_ref, blk_ref, st_ref, carry_ref, *, tr, nt):
    i = pl.program_id(0)

    @pl.when(i == 0)
    def _():
        carry_ref[...] = jnp.zeros_like(carry_ref)

    lgt = lg_ref[...].T
    neg = jnp.float32(-jnp.inf)
    row8 = lax.broadcasted_iota(jnp.int32, (8, tr), 0)
    row = lax.broadcasted_iota(jnp.int32, (N_EXPERTS, tr), 0)
    first = lambda hit, idx, n: jnp.min(jnp.where(hit, idx, n), axis=0, keepdims=True)

    gl = jnp.where(row8 < N_GROUPS, lgt[N_EXPERTS:N_EXPERTS + 8], neg)
    gmax = jnp.max(gl, axis=0, keepdims=True)
    gidx = first(gl == gmax, row8, 8)
    gprob = 1.0 / jnp.sum(jnp.exp(gl - gmax), axis=0, keepdims=True)

    el = jnp.where(row // EXPERTS_PER_GROUP == gidx, lgt[0:N_EXPERTS], neg)
    m1 = jnp.max(el, axis=0, keepdims=True)
    e0 = first(el == m1, row, N_EXPERTS)
    el2 = jnp.where(row == e0, neg, el)
    m2 = jnp.max(el2, axis=0, keepdims=True)
    e1 = first(el2 == m2, row, N_EXPERTS)
    r = jnp.exp(m2 - m1)
    g0 = gprob / (1.0 + r)
    g1 = gprob * r / (1.0 + r)

    oh0 = jnp.where(row == e0, 1.0, 0.0)
    oh1 = jnp.where(row == e1, 1.0, 0.0)
    oh = oh0 + oh1
    rr = lax.broadcasted_iota(jnp.int32, (tr, tr), 0)
    cc = lax.broadcasted_iota(jnp.int32, (tr, tr), 1)
    earlier = jnp.where(rr < cc, 1.0, 0.0).astype(BF16)
    carry = carry_ref[...]
    before = (jnp.dot(oh.astype(BF16), earlier, preferred_element_type=F32)
              + jnp.tile(carry, (1, tr // LANES)))
    rank0 = jnp.sum(before * oh0, axis=0, keepdims=True)
    rank1 = jnp.sum(before * oh1, axis=0, keepdims=True)
    carry_ref[...] = carry + jnp.sum(oh, axis=1, keepdims=True)

    st_ref[:, pl.ds(pl.multiple_of(i * tr, tr), tr)] = _rows8(
        (e0.astype(F32), e1.astype(F32), rank0, rank1), tr)
    gate_ref[...] = jnp.concatenate([_rows8((g0, g1), tr), jnp.zeros((LANES - 8, tr), F32)], axis=0).T

    @pl.when(i == nt - 1)
    def _():
        cnt = carry_ref[...]
        nblk = jnp.floor((cnt + (ROW_BLOCK - 1)) * (1.0 / ROW_BLOCK))
        er = lax.broadcasted_iota(jnp.int32, cnt.shape, 0)
        ec = lax.broadcasted_iota(jnp.int32, cnt.shape, 1)
        nblk_row = jnp.sum(jnp.where(er == ec, nblk, 0.0), axis=0, keepdims=True)
        pstart = jnp.sum(jnp.where(ec < er, nblk_row, 0.0), axis=1, keepdims=True)
        pend = pstart + nblk[:, 0:1]
        bidx = lax.broadcasted_iota(jnp.int32, (N_EXPERTS, blk_ref.shape[1]), 1).astype(F32)
        block_e = jnp.minimum(jnp.sum(jnp.where(pend <= bidx, 1.0, 0.0), axis=0, keepdims=True),
                              N_EXPERTS - 1.0)
        total = jnp.sum(nblk_row, axis=1, keepdims=True)
        blk_ref[...] = _rows8((block_e, jnp.broadcast_to(total, block_e.shape)), blk_ref.shape[1])
        rowf = row.astype(F32)

        def dests(j, c):
            sl = pl.ds(pl.multiple_of(j * tr, tr), tr)
            st = st_ref[:, sl]
            base = lambda e: ROW_BLOCK * jnp.sum(jnp.where(rowf == e, pstart, 0.0), axis=0, keepdims=True)
            plan_ref[:, sl] = _rows8((base(st[0:1]) + st[2:3], base(st[1:2]) + st[3:4]), tr)
            return c

        lax.fori_loop(0, nt, dests, 0)


def _route(logits, tr, n_blocks):
    t = logits.shape[0]
    nt = t // tr
    nbp = -(-n_blocks // LANES) * LANES
    return pl.pallas_call(
        functools.partial(_route_kernel, tr=tr, nt=nt),
        grid=(nt,),
        in_specs=[pl.BlockSpec((tr, LANES), lambda i: (i, 0))],
        out_specs=[pl.BlockSpec((tr, LANES), lambda i: (i, 0)),
                   pl.BlockSpec((8, t), lambda i: (0, 0)),
                   pl.BlockSpec((8, nbp), lambda i: (0, 0))],
        out_shape=[jax.ShapeDtypeStruct((t, LANES), F32),
                   jax.ShapeDtypeStruct((8, t), F32),
                   jax.ShapeDtypeStruct((8, nbp), F32)],
        scratch_shapes=[pltpu.VMEM((8, t), F32), pltpu.VMEM((N_EXPERTS, LANES), F32)],
        compiler_params=_cparams("arbitrary"),
        name="route",
    )(logits)


def _dispatch_kernel(d0_ref, d1_ref, h_ref, xs_in_ref, xs_ref, sem, *, ts):
    del xs_in_ref

    def issue(r, c):
        for d_ref in (d0_ref, d1_ref):
            pltpu.make_async_copy(h_ref.at[pl.ds(r, 1)], xs_ref.at[pl.ds(d_ref[0, 0, r], 1)], sem).start()
        return c

    lax.fori_loop(0, ts, issue, 0)
    for k in range(2):
        pltpu.make_async_copy(h_ref, xs_ref.at[pl.ds(0, ts)], sem).wait()


def _dest_spec(ts):
    return pl.BlockSpec((1, 1, ts), lambda i: (i, 0, 0), memory_space=pltpu.SMEM)


def _dispatch(hp, d0, d1, n_rows, ts):
    t, dh = hp.shape
    xs0 = jnp.zeros((n_rows, dh), jnp.uint32)
    return pl.pallas_call(
        functools.partial(_dispatch_kernel, ts=ts),
        grid=(t // ts,),
        in_specs=[_dest_spec(ts), _dest_spec(ts),
                  pl.BlockSpec((ts, dh), lambda i: (i, 0)),
                  pl.BlockSpec(memory_space=pl.ANY)],
        out_specs=pl.BlockSpec(memory_space=pl.ANY),
        out_shape=jax.ShapeDtypeStruct((n_rows, dh), jnp.uint32),
        scratch_shapes=[pltpu.SemaphoreType.DMA(())],
        input_output_aliases={3: 0},
        compiler_params=_cparams("arbitrary"),
        name="dispatch",
    )(d0, d1, hp, xs0)


def _expert_kernel(be_ref, nu_ref, xs_ref, wg_ref, wu_ref, wd_ref, ys_ref, wg_s, wu_s, wd_s):
    b = pl.program_id(0)

    @pl.when(b < nu_ref[0])
    def _():
        prev = be_ref[jnp.maximum(b - 1, 0)]

        @pl.when((b == 0) | (be_ref[b] != prev))
        def _():
            wg_s[...] = wg_ref[0, 0].astype(BF16)
            wu_s[...] = wu_ref[0, 0].astype(BF16)
            wd_s[...] = wd_ref[0, 0].astype(BF16)

        xa, xb = _unpack_bf16_pairs(xs_ref[...])
        half = xa.shape[1]
        mm = lambda a, w: jnp.dot(a, w, preferred_element_type=F32)
        gt = mm(xa, wg_s[0:half, :]) + mm(xb, wg_s[half:, :])
        up = mm(xa, wu_s[0:half, :]) + mm(xb, wu_s[half:, :])
        act = gt * (1.0 / (1.0 + jnp.exp(-gt))) * up
        ys_ref[...] = mm(act.astype(BF16), wd_s[...])

    @pl.when(b >= nu_ref[0])
    def _():
        ys_ref[...] = jnp.zeros_like(ys_ref)


def _experts(xs, block_e, nb_used, w_gate, w_up, w_down, layer):
    n_rows, dh = xs.shape
    n_blocks = n_rows // ROW_BLOCK
    _, _, d, de = w_gate.shape
    blk = lambda b, be, nu: (jnp.minimum(b, nu[0] - 1), 0)
    wsel = lambda b, be, nu: (layer, be[jnp.minimum(b, nu[0] - 1)], 0, 0)
    return pl.pallas_call(
        _expert_kernel,
        grid_spec=pltpu.PrefetchScalarGridSpec(
            num_scalar_prefetch=2,
            grid=(n_blocks,),
            in_specs=[pl.BlockSpec((ROW_BLOCK, dh), blk),
                      pl.BlockSpec((1, 1, d, de), wsel),
                      pl.BlockSpec((1, 1, d, de), wsel),
                      pl.BlockSpec((1, 1, de, d), wsel)],
            out_specs=pl.BlockSpec((ROW_BLOCK, d), lambda b, be, nu: (b, 0)),
            scratch_shapes=[pltpu.VMEM((d, de), BF16), pltpu.VMEM((d, de), BF16),
                            pltpu.VMEM((de, d), BF16)]),
        out_shape=jax.ShapeDtypeStruct((n_rows, d), F32),
        compiler_params=_cparams("arbitrary"),
        name="experts",
    )(block_e, nb_used, xs, w_gate, w_up, w_down)


def _combine_kernel(d0_ref, d1_ref, x_ref, gate_ref, ys_ref, fin_ref, out_ref, ybuf, sem, *, ts, final):
    def issue(r, c):
        for k, d_ref in enumerate((d0_ref, d1_ref)):
            pltpu.make_async_copy(ys_ref.at[pl.ds(d_ref[0, 0, r], 1)], ybuf.at[k, pl.ds(r, 1)], sem).start()
        return c

    lax.fori_loop(0, ts, issue, 0)
    for k in range(2):
        pltpu.make_async_copy(ys_ref.at[pl.ds(0, ts)], ybuf.at[k], sem).wait()
    gate = gate_ref[...]
    out = x_ref[...] + gate[:, 0:1] * ybuf[0] + gate[:, 1:2] * ybuf[1]
    if final:
        out = _rms(out, fin_ref[...])
    out_ref[...] = out


def _combine(x2d, gate, d0, d1, ys, fin, ts, final):
    t, d = x2d.shape
    return pl.pallas_call(
        functools.partial(_combine_kernel, ts=ts, final=final),
        grid=(t // ts,),
        in_specs=[_dest_spec(ts), _dest_spec(ts),
                  pl.BlockSpec((ts, d), lambda i: (i, 0)),
                  pl.BlockSpec((ts, LANES), lambda i: (i, 0)),
                  pl.BlockSpec(memory_space=pl.ANY),
                  pl.BlockSpec((1, d), lambda i: (0, 0))],
        out_specs=pl.BlockSpec((ts, d), lambda i: (i, 0)),
        out_shape=jax.ShapeDtypeStruct((t, d), F32),
        scratch_shapes=[pltpu.VMEM((2, ts, d), F32), pltpu.SemaphoreType.DMA(())],
        compiler_params=_cparams("arbitrary"),
        name="combine",
    )(d0, d1, x2d, gate, ys, fin)


def _moe(x2d, hp, logits, w_gate, w_up, w_down, layer, fin, final, ts):
    t = x2d.shape[0]
    n_blocks = (2 * t) // ROW_BLOCK + N_EXPERTS
    gate, plan, blk = _route(logits, min(512, t), n_blocks)
    d0 = plan[0].astype(jnp.int32).reshape(t // ts, 1, ts)
    d1 = plan[1].astype(jnp.int32).reshape(t // ts, 1, ts)
    block_e = blk[0, :n_blocks].astype(jnp.int32)
    nb_used = blk[1, 0:1].astype(jnp.int32)
    xs = _dispatch(hp, d0, d1, n_blocks * ROW_BLOCK, ts)
    ys = _experts(xs, block_e, nb_used, w_gate, w_up, w_down, layer)
    return _combine(x2d, gate, d0, d1, ys, fin, ts, final)


def _rope_pairs(t, cs):
    p = t * cs
    return p + pltpu.roll(p, QK_ROPE, axis=1)


def _qkv_kernel(x_ref, kn_ref, an_ref, wdkv_ref, kvn_ref, wuk_ref, wuv_ref, wqd_ref, qn_ref, wqu_ref,
                cs_ref, k_ref, v_ref, q_ref):
    x = x_ref[...]
    xn = x * lax.rsqrt(jnp.mean(x * x, axis=-1, keepdims=True) + EPS)
    cs = cs_ref[...]
    mm = lambda a, w: jnp.dot(a.astype(BF16), w, preferred_element_type=F32)
    lane = lax.broadcasted_iota(jnp.int32, cs.shape, 1)

    ckv = mm(xn * kn_ref[...], wdkv_ref[...])
    c_kv = _rms(ckv[:, :KV_RANK], kvn_ref[...])
    k_rope = jnp.where(lane < QK_ROPE, _rope_pairs(ckv[:, KV_RANK:], cs), 0.0).astype(BF16)
    k_nope = mm(c_kv, wuk_ref[...]).astype(BF16)
    v_ref[0] = mm(c_kv, wuv_ref[...]).astype(BF16)

    cq = _rms(mm(xn * an_ref[...], wqd_ref[...]), qn_ref[...])
    q = mm(cq, wqu_ref[...]) * ATTN_SCALE
    nope_w = N_HEADS * QK_NOPE
    for h in range(N_HEADS):
        hs = slice(h * QK_NOPE, (h + 1) * QK_NOPE)
        k_ref[0, h] = jnp.concatenate([k_nope[:, hs], k_rope], axis=-1)
        rs = slice(nope_w + h * LANES, nope_w + (h + 1) * LANES)
        q_ref[0, h] = jnp.concatenate([q[:, hs], _rope_pairs(q[:, rs], cs)], axis=-1).astype(BF16)


def _qkv(x2d, batch, seq, kn, an, wdkv, kvn, wuk, wuv, wqd, qn, wqu, cs, ts):
    t, d = x2d.shape
    tps = seq // ts
    full = lambda a: pl.BlockSpec(a.shape, lambda i: (0,) * a.ndim)
    hspec = pl.BlockSpec((1, N_HEADS, ts, QK_PAD), lambda i: (i // tps, 0, i % tps, 0))
    return pl.pallas_call(
        _qkv_kernel,
        grid=(t // ts,),
        in_specs=[pl.BlockSpec((ts, d), lambda i: (i, 0)),
                  full(kn), full(an), full(wdkv), full(kvn), full(wuk), full(wuv), full(wqd), full(qn),
                  full(wqu), pl.BlockSpec((ts, LANES), lambda i: (i % tps, 0))],
        out_specs=[hspec,
                   pl.BlockSpec((1, ts, N_HEADS * V_DIM), lambda i: (i // tps, i % tps, 0)),
                   hspec],
        out_shape=[jax.ShapeDtypeStruct((batch, N_HEADS, seq, QK_PAD), BF16),
                   jax.ShapeDtypeStruct((batch, seq, N_HEADS * V_DIM), BF16),
                   jax.ShapeDtypeStruct((batch, N_HEADS, seq, QK_PAD), BF16)],
        compiler_params=_cparams("arbitrary"),
        name="qkv_proj",
    )(x2d, kn, an, wdkv, kvn, wuk, wuv, wqd, qn, wqu, cs)


def _attn_kernel(q_ref, k_ref, v_ref, o_ref, *, tq):
    i = pl.program_id(2)
    q = q_ref[0, 0]

    def tile(j, carry, masked):
        m, l, acc = carry
        start = pl.multiple_of(j * tq, tq)
        k = k_ref[0, 0, pl.ds(start, tq), :]
        v = v_ref[0, pl.ds(start, tq), :]
        s = lax.dot_general(q, k, (((1,), (1,)), ((), ())), preferred_element_type=F32)
        if masked:
            rq = lax.broadcasted_iota(jnp.int32, s.shape, 0) // CHUNK
            ck = lax.broadcasted_iota(jnp.int32, s.shape, 1) // CHUNK
            s = jnp.where(ck <= rq, s, NEG_BIG)
        m_new = jnp.maximum(m, jnp.max(s, axis=-1, keepdims=True))
        p = jnp.exp(s - m_new)
        alpha = jnp.exp(m - m_new)
        l = alpha * l + jnp.sum(p, axis=-1, keepdims=True)
        acc = alpha * acc + jnp.dot(p.astype(BF16), v, preferred_element_type=F32)
        return m_new, l, acc

    init = (jnp.full((tq, 1), NEG_BIG, F32), jnp.zeros((tq, 1), F32), jnp.zeros((tq, V_DIM), F32))
    carry = lax.fori_loop(0, i, lambda j, c: tile(j, c, False), init)
    _, l, acc = tile(i, carry, True)
    o_ref[0] = (acc / l).astype(BF16)


def _attention(qc, kc, v, tq):
    batch, _, seq, _ = qc.shape
    return pl.pallas_call(
        functools.partial(_attn_kernel, tq=tq),
        grid=(batch, N_HEADS, seq // tq),
        in_specs=[pl.BlockSpec((1, 1, tq, QK_PAD), lambda b, h, i: (b, h, i, 0)),
                  pl.BlockSpec((1, 1, seq, QK_PAD), lambda b, h, i: (b, h, 0, 0)),
                  pl.BlockSpec((1, seq, V_DIM), lambda b, h, i: (b, 0, h))],
        out_specs=pl.BlockSpec((1, tq, V_DIM), lambda b, h, i: (b, i, h)),
        out_shape=jax.ShapeDtypeStruct((batch, seq, N_HEADS * V_DIM), BF16),
        compiler_params=_cparams("arbitrary", "arbitrary", "arbitrary"),
        name="attention",
    )(qc, kc, v)


def _oproj_kernel(o_ref, x_ref, wo_ref, fn_ref, wr_ref, br_ref, x3_ref, hp_ref, lg_ref):
    x3 = x_ref[...] + jnp.dot(o_ref[...], wo_ref[...], preferred_element_type=F32)
    x3_ref[...] = x3
    _ffn_prologue(x3, fn_ref[...], wr_ref[...], br_ref[...], hp_ref, lg_ref)


def _oproj(o2d, x2d, wo, fn, wr, br, ts):
    t, d = x2d.shape
    full = lambda a: pl.BlockSpec(a.shape, lambda i: (0,) * a.ndim)
    return pl.pallas_call(
        _oproj_kernel,
        grid=(t // ts,),
        in_specs=[pl.BlockSpec((ts, o2d.shape[1]), lambda i: (i, 0)),
                  pl.BlockSpec((ts, d), lambda i: (i, 0)),
                  full(wo), full(fn), full(wr), full(br)],
        out_specs=[pl.BlockSpec((ts, d), lambda i: (i, 0)),
                   pl.BlockSpec((ts, d // 2), lambda i: (i, 0)),
                   pl.BlockSpec((ts, LANES), lambda i: (i, 0))],
        out_shape=[jax.ShapeDtypeStruct((t, d), F32),
                   jax.ShapeDtypeStruct((t, d // 2), jnp.uint32),
                   jax.ShapeDtypeStruct((t, LANES), F32)],
        compiler_params=_cparams("arbitrary"),
        name="out_proj",
    )(o2d, x2d, wo, fn, wr, br)


def _router_params(rg_w, rg_b, re_w, re_b):
    d = rg_w.shape[0]
    used = N_GROUPS + N_EXPERTS
    wr = jnp.concatenate([re_w, rg_w, jnp.zeros((d, LANES - used), F32)], axis=1)
    br = jnp.concatenate([re_b, rg_b, jnp.zeros((LANES - used,), F32)])[None, :]
    return wr, br


def _with_rotate_half(w):
    half = w.shape[-1] // 2
    return jnp.concatenate([w, -w[..., half:], w[..., :half]], axis=-1)


def _rope_table(seq):
    half = QK_ROPE // 2
    inv = ROPE_THETA ** (-jnp.arange(half, dtype=F32) / half)
    ang = jnp.arange(seq, dtype=F32)[:, None] * inv[None, :]
    cos, sin = jnp.cos(ang), jnp.sin(ang)
    return jnp.concatenate([cos, cos, sin, sin], axis=1)


def kernel(x, pool_norm, pool_w, pool_b, pool_scale, kv_in_norm, w_dkv, kv_norm, w_uk, w_uv, attn_norm, wq_down, q_norm, wq_up, wo, ffn_norm, router_group_w, router_group_b, router_expert_w, router_expert_b, w_gate, w_up, w_down, final_norm):
    batch, seq, d = x.shape
    t = batch * seq
    depth = ffn_norm.shape[0]
    n_a = pool_norm.shape[0]
    ts = min(256, seq)
    tq = min(512, seq)
    row = lambda a: a.reshape(1, -1)

    cs = _rope_table(seq)
    routers = [_router_params(router_group_w[l], router_group_b[l], router_expert_w[l], router_expert_b[l])
               for l in range(depth)]
    wdkv = jnp.concatenate([w_dkv[:, :KV_RANK], _with_rotate_half(w_dkv[:, KV_RANK:])], axis=1).astype(BF16)
    wuk, wuv = w_uk.astype(BF16), w_uv.astype(BF16)

    x2d = x.reshape(t, d)
    kc = vv = None
    for l in range(depth):
        wr, br = routers[l]
        if l < n_a:
            x2d, hp, logits = _pool_layer(x2d, seq, row(pool_norm[l]), pool_w[l].astype(BF16), pool_b[l][:, None, :],
                                          row(pool_scale[l]), row(ffn_norm[l]), wr, br, ts)
        else:
            j = l - n_a
            wqu = wq_up[j].reshape(-1, N_HEADS, QK_NOPE + QK_ROPE)
            wqu = jnp.concatenate([wqu[:, :, :QK_NOPE].reshape(-1, N_HEADS * QK_NOPE),
                                   _with_rotate_half(wqu[:, :, QK_NOPE:]).reshape(-1, N_HEADS * LANES)],
                                  axis=1).astype(BF16)
            k_new, v_new, qc = _qkv(x2d, batch, seq, row(kv_in_norm), row(attn_norm[j]), wdkv, row(kv_norm),
                                    wuk, wuv, wq_down[j].astype(BF16), row(q_norm[j]), wqu, cs, ts)
            if kc is None:
                kc, vv = k_new, v_new
            o = _attention(qc, kc, vv, tq)
            x2d, hp, logits = _oproj(o.reshape(t, -1), x2d, wo[j].astype(BF16), row(ffn_norm[l]), wr, br, ts)
        x2d = _moe(x2d, hp, logits, w_gate, w_up, w_down, l, row(final_norm), l == depth - 1, ts)
    return x2d.reshape(batch, seq, d)
```

```python
import functools
import math

import jax
import jax.numpy as jnp
from jax import lax
from jax.experimental import pallas as pl
from jax.experimental.pallas import tpu as pltpu

EPS = 1e-6
CHUNK = 64
POOL_WINDOWS = (2, 4, 8, 16)
N_HEADS = 8
QK_NOPE = 128
QK_ROPE = 64
V_DIM = 128
KV_RANK = 256
ROPE_THETA = 10000.0
ATTN_SCALE = 1.0 / math.sqrt(QK_NOPE + QK_ROPE)
N_GROUPS = 4
EXPERTS_PER_GROUP = 8
N_EXPERTS = N_GROUPS * EXPERTS_PER_GROUP
ROW_BLOCK = 256

LANES = 128
HALO = 16
QK_PAD = 256
VT_PAD = V_DIM + 16
LOG2E = math.log2(math.e)
VMEM_LIMIT = 56 * 1024 * 1024
NEG_BIG = -1e30

F32 = jnp.float32
BF16 = jnp.bfloat16


def _cparams(*sem):
    return pltpu.CompilerParams(dimension_semantics=sem, vmem_limit_bytes=VMEM_LIMIT)


def _rms(x, g):
    return x * lax.rsqrt(jnp.mean(x * x, axis=-1, keepdims=True) + EPS) * g


def _pack_bf16_pairs(h):
    m = h.shape[1] // 2
    bits = pltpu.bitcast(h.astype(BF16).astype(F32), jnp.uint32)
    return (bits[:, :m] >> 16) | bits[:, m:]


def _unpack_bf16_pairs(w):
    lo = pltpu.bitcast(w << 16, F32).astype(BF16)
    hi = pltpu.bitcast(w & jnp.uint32(0xFFFF0000), F32).astype(BF16)
    return lo, hi


def _ffn_prologue(x, fn, wr, br, hp_ref, lg_ref):
    h = _rms(x, fn)
    hp_ref[...] = _pack_bf16_pairs(h)
    lg_ref[...] = jnp.dot(h, wr, precision=lax.Precision.HIGHEST,
                          preferred_element_type=F32) + br


def _pool_kernel(x_ref, pn_ref, pw_ref, pb_ref, ps_ref, fn_ref, wr_ref, br_ref,
                 x1_ref, hp_ref, lg_ref, buf_ref, *, ts, tiles_per_seq):
    i = pl.program_id(0)
    seq_tile = i % tiles_per_seq
    x = x_ref[...]
    h = _rms(x, pn_ref[...])

    @pl.when(seq_tile == 0)
    def _():
        buf_ref[0:HALO, :] = jnp.zeros((HALO, x.shape[1]), F32)

    buf_ref[HALO:, :] = h
    pos = lax.broadcasted_iota(jnp.int32, (ts, 1), 0) + seq_tile * ts
    gd = x.shape[1] // len(POOL_WINDOWS)
    for g, win in enumerate(POOL_WINDOWS):
        cols = slice(g * gd, (g + 1) * gd)
        s = buf_ref[:, cols]
        k = 1
        while k < win:
            s = s + pltpu.roll(s, k, axis=0)
            k *= 2
        cnt = jnp.minimum(pos + 1, win).astype(F32)
        pooled = s[HALO:, :] / cnt - h[:, cols]
        y = jnp.dot(pooled.astype(BF16), pw_ref[g], preferred_element_type=F32) + pb_ref[g]
        x1_ref[:, cols] = x[:, cols] + y * ps_ref[:, cols]
    buf_ref[0:HALO, :] = h[ts - HALO:, :]
    _ffn_prologue(x1_ref[...], fn_ref[...], wr_ref[...], br_ref[...], hp_ref, lg_ref)


def _pool_layer(x2d, seq, pn, pw, pb, ps, fn, wr, br, ts):
    t, d = x2d.shape
    full = lambda *shape: pl.BlockSpec(shape, lambda i: (0,) * len(shape))
    return pl.pallas_call(
        functools.partial(_pool_kernel, ts=ts, tiles_per_seq=seq // ts),
        grid=(t // ts,),
        in_specs=[pl.BlockSpec((ts, d), lambda i: (i, 0)),
                  full(1, d), full(*pw.shape), full(*pb.shape), full(1, d), full(1, d),
                  full(d, LANES), full(1, LANES)],
        out_specs=[pl.BlockSpec((ts, d), lambda i: (i, 0)),
                   pl.BlockSpec((ts, d // 2), lambda i: (i, 0)),
                   pl.BlockSpec((ts, LANES), lambda i: (i, 0))],
        out_shape=[jax.ShapeDtypeStruct((t, d), F32),
                   jax.ShapeDtypeStruct((t, d // 2), jnp.uint32),
                   jax.ShapeDtypeStruct((t, LANES), F32)],
        scratch_shapes=[pltpu.VMEM((HALO + ts, d), F32)],
        compiler_params=_cparams("arbitrary"),
        name="pool_layer",
    )(x2d, pn, pw, pb, ps, fn, wr, br)


def _rows8(vals, width):
    row8 = lax.broadcasted_iota(jnp.int32, (8, width), 0)
    out = jnp.zeros((8, width), F32)
    for c, v in enumerate(vals):
        out = jnp.where(row8 == c, v, out)
    return out


def _route_kernel(lg_ref, gate_ref, plan_ref, blk_ref, st_ref, carry_ref, *, tr, nt):
    i = pl.program_id(0)

    @pl.when(i == 0)
    def _():
        carry_ref[...] = jnp.zeros_like(carry_ref)

    lgt = lg_ref[...].T
    neg = jnp.float32(-jnp.inf)
    row8 = lax.broadcasted_iota(jnp.int32, (8, tr), 0)
    row = lax.broadcasted_iota(jnp.int32, (N_EXPERTS, tr), 0)
    first = lambda hit, idx, n: jnp.min(jnp.where(hit, idx, n), axis=0, keepdims=True)

    gl = jnp.where(row8 < N_GROUPS, lgt[N_EXPERTS:N_EXPERTS + 8], neg)
    gmax = jnp.max(gl, axis=0, keepdims=True)
    gidx = first(gl == gmax, row8, 8)
    gprob = 1.0 / jnp.sum(jnp.exp(gl - gmax), axis=0, keepdims=True)

    el = jnp.where(row // EXPERTS_PER_GROUP == gidx, lgt[0:N_EXPERTS], neg)
    m1 = jnp.max(el, axis=0, keepdims=True)
    e0 = first(el == m1, row, N_EXPERTS)
    el2 = jnp.where(row == e0, neg, el)
    m2 = jnp.max(el2, axis=0, keepdims=True)
    e1 = first(el2 == m2, row, N_EXPERTS)
    r = jnp.exp(m2 - m1)
    g0 = gprob / (1.0 + r)
    g1 = gprob * r / (1.0 + r)

    oh0 = jnp.where(row == e0, 1.0, 0.0)
    oh1 = jnp.where(row == e1, 1.0, 0.0)
    oh = oh0 + oh1
    rr = lax.broadcasted_iota(jnp.int32, (tr, tr), 0)
    cc = lax.broadcasted_iota(jnp.int32, (tr, tr), 1)
    earlier = jnp.where(rr < cc, 1.0, 0.0).astype(BF16)
    carry = carry_ref[...]
    before = (jnp.dot(oh.astype(BF16), earlier, preferred_element_type=F32)
              + jnp.tile(carry, (1, tr // LANES)))
    rank0 = jnp.sum(before * oh0, axis=0, keepdims=True)
    rank1 = jnp.sum(before * oh1, axis=0, keepdims=True)
    carry_ref[...] = carry + jnp.sum(oh, axis=1, keepdims=True)

    st_ref[:, pl.ds(pl.multiple_of(i * tr, tr), tr)] = _rows8(
        (e0.astype(F32), e1.astype(F32), rank0, rank1), tr)
    gate_ref[...] = jnp.concatenate([_rows8((g0, g1), tr), jnp.zeros((LANES - 8, tr), F32)], axis=0).T

    @pl.when(i == nt - 1)
    def _():
        cnt = carry_ref[...]
        nblk = jnp.floor((cnt + (ROW_BLOCK - 1)) * (1.0 / ROW_BLOCK))
        er = lax.broadcasted_iota(jnp.int32, cnt.shape, 0)
        ec = lax.broadcasted_iota(jnp.int32, cnt.shape, 1)
        nblk_row = jnp.sum(jnp.where(er == ec, nblk, 0.0), axis=0, keepdims=True)
        pstart = jnp.sum(jnp.where(ec < er, nblk_row, 0.0), axis=1, keepdims=True)
        pend = pstart + nblk[:, 0:1]
        bidx = lax.broadcasted_iota(jnp.int32, (N_EXPERTS, blk_ref.shape[1]), 1).astype(F32)
        block_e = jnp.minimum(jnp.sum(jnp.where(pend <= bidx, 1.0, 0.0), axis=0, keepdims=True),
                              N_EXPERTS - 1.0)
        total = jnp.sum(nblk_row, axis=1, keepdims=True)
        blk_ref[...] = _rows8((block_e, jnp.broadcast_to(total, block_e.shape)), blk_ref.shape[1])
        rowf = row.astype(F32)

        def dests(j, c):
            sl = pl.ds(pl.multiple_of(j * tr, tr), tr)
            st = st_ref[:, sl]
            base = lambda e: ROW_BLOCK * jnp.sum(jnp.where(rowf == e, pstart, 0.0), axis=0, keepdims=True)
            plan_ref[:, sl] = _rows8((base(st[0:1]) + st[2:3], base(st[1:2]) + st[3:4]), tr)
            return c

        lax.fori_loop(0, nt, dests, 0)


def _route(logits, tr, n_blocks):
    t = logits.shape[0]
    nt = t // tr
    nbp = -(-n_blocks // LANES) * LANES
    return pl.pallas_call(
        functools.partial(_route_kernel, tr=tr, nt=nt),
        grid=(nt,),
        in_specs=[pl.BlockSpec((tr, LANES), lambda i: (i, 0))],
        out_specs=[pl.BlockSpec((tr, LANES), lambda i: (i, 0)),
                   pl.BlockSpec((8, t), lambda i: (0, 0)),
                   pl.BlockSpec((8, nbp), lambda i: (0, 0))],
        out_shape=[jax.ShapeDtypeStruct((t, LANES), F32),
                   jax.ShapeDtypeStruct((8, t), F32),
                   jax.ShapeDtypeStruct((8, nbp), F32)],
        scratch_shapes=[pltpu.VMEM((8, t), F32), pltpu.VMEM((N_EXPERTS, LANES), F32)],
        compiler_params=_cparams("arbitrary"),
        name="route",
    )(logits)


def _dispatch_kernel(d0_ref, d1_ref, h_ref, xs_in_ref, xs_ref, sem, *, ts):
    del xs_in_ref

    def issue(r, c):
        for d_ref in (d0_ref, d1_ref):
            pltpu.make_async_copy(h_ref.at[pl.ds(r, 1)], xs_ref.at[pl.ds(d_ref[0, 0, r], 1)], sem).start()
        return c

    lax.fori_loop(0, ts, issue, 0)
    for k in range(2):
        pltpu.make_async_copy(h_ref, xs_ref.at[pl.ds(0, ts)], sem).wait()


def _dest_spec(ts):
    return pl.BlockSpec((1, 1, ts), lambda i: (i, 0, 0), memory_space=pltpu.SMEM)


def _dispatch(hp, d0, d1, n_rows, ts):
    t, dh = hp.shape
    xs0 = jnp.zeros((n_rows, dh), jnp.uint32)
    return pl.pallas_call(
        functools.partial(_dispatch_kernel, ts=ts),
        grid=(t // ts,),
        in_specs=[_dest_spec(ts), _dest_spec(ts),
                  pl.BlockSpec((ts, dh), lambda i: (i, 0)),
                  pl.BlockSpec(memory_space=pl.ANY)],
        out_specs=pl.BlockSpec(memory_space=pl.ANY),
        out_shape=jax.ShapeDtypeStruct((n_rows, dh), jnp.uint32),
        scratch_shapes=[pltpu.SemaphoreType.DMA(())],
        input_output_aliases={3: 0},
        compiler_params=_cparams("arbitrary"),
        name="dispatch",
    )(d0, d1, hp, xs0)


def _expert_kernel(be_ref, nu_ref, xs_ref, wg_ref, wu_ref, wd_ref, ys_ref, wg_s, wu_s, wd_s):
    b = pl.program_id(0)

    @pl.when(b < nu_ref[0])
    def _():
        prev = be_ref[jnp.maximum(b - 1, 0)]

        @pl.when((b == 0) | (be_ref[b] != prev))
        def _():
            wg_s[...] = wg_ref[0, 0].astype(BF16)
            wu_s[...] = wu_ref[0, 0].astype(BF16)
            wd_s[...] = wd_ref[0, 0].astype(BF16)

        xa, xb = _unpack_bf16_pairs(xs_ref[...])
        half = xa.shape[1]
        mm = lambda a, w: jnp.dot(a, w, preferred_element_type=F32)
        gt = mm(xa, wg_s[0:half, :]) + mm(xb, wg_s[half:, :])
        up = mm(xa, wu_s[0:half, :]) + mm(xb, wu_s[half:, :])
        act = gt * (1.0 / (1.0 + jnp.exp(-gt))) * up
        ys_ref[...] = mm(act.astype(BF16), wd_s[...])

    @pl.when(b >= nu_ref[0])
    def _():
        ys_ref[...] = jnp.zeros_like(ys_ref)


def _experts(xs, block_e, nb_used, w_gate, w_up, w_down, layer):
    n_rows, dh = xs.shape
    n_blocks = n_rows // ROW_BLOCK
    _, _, d, de = w_gate.shape
    blk = lambda b, be, nu: (jnp.minimum(b, nu[0] - 1), 0)
    wsel = lambda b, be, nu: (layer, be[jnp.minimum(b, nu[0] - 1)], 0, 0)
    return pl.pallas_call(
        _expert_kernel,
        grid_spec=pltpu.PrefetchScalarGridSpec(
            num_scalar_prefetch=2,
            grid=(n_blocks,),
            in_specs=[pl.BlockSpec((ROW_BLOCK, dh), blk),
                      pl.BlockSpec((1, 1, d, de), wsel),
                      pl.BlockSpec((1, 1, d, de), wsel),
                      pl.BlockSpec((1, 1, de, d), wsel)],
            out_specs=pl.BlockSpec((ROW_BLOCK, d), lambda b, be, nu: (b, 0)),
            scratch_shapes=[pltpu.VMEM((d, de), BF16), pltpu.VMEM((d, de), BF16),
                            pltpu.VMEM((de, d), BF16)]),
        out_shape=jax.ShapeDtypeStruct((n_rows, d), F32),
        compiler_params=_cparams("arbitrary"),
        name="experts",
    )(block_e, nb_used, xs, w_gate, w_up, w_down)


def _combine_kernel(d0_ref, d1_ref, x_ref, gate_ref, ys_ref, fin_ref, out_ref, ybuf, sem, *, ts, final):
    def issue(r, c):
        for k, d_ref in enumerate((d0_ref, d1_ref)):
            pltpu.make_async_copy(ys_ref.at[pl.ds(d_ref[0, 0, r], 1)], ybuf.at[k, pl.ds(r, 1)], sem).start()
        return c

    lax.fori_loop(0, ts, issue, 0)
    for k in range(2):
        pltpu.make_async_copy(ys_ref.at[pl.ds(0, ts)], ybuf.at[k], sem).wait()
    gate = gate_ref[...]
    out = x_ref[...] + gate[:, 0:1] * ybuf[0] + gate[:, 1:2] * ybuf[1]
    if final:
        out = _rms(out, fin_ref[...])
    out_ref[...] = out


def _combine(x2d, gate, d0, d1, ys, fin, ts, final):
    t, d = x2d.shape
    return pl.pallas_call(
        functools.partial(_combine_kernel, ts=ts, final=final),
        grid=(t // ts,),
        in_specs=[_dest_spec(ts), _dest_spec(ts),
                  pl.BlockSpec((ts, d), lambda i: (i, 0)),
                  pl.BlockSpec((ts, LANES), lambda i: (i, 0)),
                  pl.BlockSpec(memory_space=pl.ANY),
                  pl.BlockSpec((1, d), lambda i: (0, 0))],
        out_specs=pl.BlockSpec((ts, d), lambda i: (i, 0)),
        out_shape=jax.ShapeDtypeStruct((t, d), F32),
        scratch_shapes=[pltpu.VMEM((2, ts, d), F32), pltpu.SemaphoreType.DMA(())],
        compiler_params=_cparams("arbitrary"),
        name="combine",
    )(d0, d1, x2d, gate, ys, fin)


def _moe(x2d, hp, logits, w_gate, w_up, w_down, layer, fin, final, ts):
    t = x2d.shape[0]
    n_blocks = (2 * t) // ROW_BLOCK + N_EXPERTS
    gate, plan, blk = _route(logits, min(512, t), n_blocks)
    d0 = plan[0].astype(jnp.int32).reshape(t // ts, 1, ts)
    d1 = plan[1].astype(jnp.int32).reshape(t // ts, 1, ts)
    block_e = blk[0, :n_blocks].astype(jnp.int32)
    nb_used = blk[1, 0:1].astype(jnp.int32)
    xs = _dispatch(hp, d0, d1, n_blocks * ROW_BLOCK, ts)
    ys = _experts(xs, block_e, nb_used, w_gate, w_up, w_down, layer)
    return _combine(x2d, gate, d0, d1, ys, fin, ts, final)


def _rope_pairs(t, cs):
    p = t * cs
    return p + pltpu.roll(p, QK_ROPE, axis=1)


def _qkv_kernel(x_ref, kn_ref, an_ref, wdkv_ref, kvn_ref, wuk_ref, wuvt_ref, wqd_ref, qn_ref, wqut_ref,
                cs_ref, cst_ref, k_ref, vt_ref, qt_ref):
    x = x_ref[...]
    xn = x * lax.rsqrt(jnp.mean(x * x, axis=-1, keepdims=True) + EPS)
    cs = cs_ref[...]
    mm = lambda a, w: jnp.dot(a.astype(BF16), w, preferred_element_type=F32)
    lane = lax.broadcasted_iota(jnp.int32, cs.shape, 1)

    ckv = mm(xn * kn_ref[...], wdkv_ref[...])
    c_kv = _rms(ckv[:, :KV_RANK], kvn_ref[...])
    k_rope = jnp.where(lane < QK_ROPE, _rope_pairs(ckv[:, KV_RANK:], cs), 0.0).astype(BF16)
    k_nope = mm(c_kv, wuk_ref[...]).astype(BF16)
    for h in range(N_HEADS):
        k_ref[0, h] = jnp.concatenate([k_nope[:, h * QK_NOPE:(h + 1) * QK_NOPE], k_rope], axis=-1)

    mmt = lambda wt, a: lax.dot_general(wt, a.astype(BF16), (((1,), (1,)), ((), ())),
                                        preferred_element_type=F32)
    ts = x.shape[0]
    vt = mmt(wuvt_ref[...], c_kv).astype(BF16)
    one_row = jnp.where(lax.broadcasted_iota(jnp.int32, (VT_PAD - V_DIM, ts), 0) == 0, 1.0, 0.0).astype(BF16)
    cq = _rms(mm(xn * an_ref[...], wqd_ref[...]), qn_ref[...])
    qt = mmt(wqut_ref[...], cq) * (ATTN_SCALE * LOG2E)
    cst = cst_ref[...]
    nope_w = N_HEADS * QK_NOPE
    zpad = jnp.zeros((QK_PAD - QK_NOPE - QK_ROPE, ts), F32)
    for h in range(N_HEADS):
        vt_ref[0, h] = jnp.concatenate([vt[h * V_DIM:(h + 1) * V_DIM], one_row], axis=0)
        rp = qt[nope_w + h * LANES:nope_w + (h + 1) * LANES] * cst
        rope = rp[:QK_ROPE] + rp[QK_ROPE:]
        qt_ref[0, h] = jnp.concatenate([qt[h * QK_NOPE:(h + 1) * QK_NOPE], rope, zpad], axis=0).astype(BF16)


def _qkv(x2d, batch, seq, kn, an, wdkv, kvn, wuk, wuvt, wqd, qn, wqut, cs, cst, ts):
    t, d = x2d.shape
    tps = seq // ts
    full = lambda a: pl.BlockSpec(a.shape, lambda i: (0,) * a.ndim)
    tspec = lambda rows: pl.BlockSpec((1, N_HEADS, rows, ts), lambda i: (i // tps, 0, 0, i % tps))
    return pl.pallas_call(
        _qkv_kernel,
        grid=(t // ts,),
        in_specs=[pl.BlockSpec((ts, d), lambda i: (i, 0)),
                  full(kn), full(an), full(wdkv), full(kvn), full(wuk), full(wuvt), full(wqd), full(qn),
                  full(wqut), pl.BlockSpec((ts, LANES), lambda i: (i % tps, 0)),
                  pl.BlockSpec((LANES, ts), lambda i: (0, i % tps))],
        out_specs=[pl.BlockSpec((1, N_HEADS, ts, QK_PAD), lambda i: (i // tps, 0, i % tps, 0)),
                   tspec(VT_PAD), tspec(QK_PAD)],
        out_shape=[jax.ShapeDtypeStruct((batch, N_HEADS, seq, QK_PAD), BF16),
                   jax.ShapeDtypeStruct((batch, N_HEADS, VT_PAD, seq), BF16),
                   jax.ShapeDtypeStruct((batch, N_HEADS, QK_PAD, seq), BF16)],
        compiler_params=_cparams("arbitrary"),
        name="qkv_proj",
    )(x2d, kn, an, wdkv, kvn, wuk, wuvt, wqd, qn, wqut, cs, cst)


def _attn_kernel(qt_ref, k_ref, vt_ref, o_ref, acc_ref, *, tq):
    i = pl.program_id(2)
    hps = qt_ref.shape[1]
    acc_ref[...] = jnp.zeros_like(acc_ref)

    def tile(j, ms, masked):
        start = pl.multiple_of(j * tq, tq)
        scores = [jnp.dot(k_ref[0, h, pl.ds(start, tq), :], qt_ref[0, h], preferred_element_type=F32)
                  for h in range(hps)]
        out = []
        for h in range(hps):
            vt = vt_ref[0, h, :, pl.ds(start, tq)]
            s = scores[h]
            if masked:
                kc = lax.broadcasted_iota(jnp.int32, s.shape, 0) // CHUNK
                qc = lax.broadcasted_iota(jnp.int32, s.shape, 1) // CHUNK
                s = jnp.where(kc <= qc, s, NEG_BIG)
            m_new = jnp.maximum(ms[h], jnp.max(s, axis=0, keepdims=True))
            p = jnp.exp2((s - m_new).astype(BF16))
            acc_ref[h] = jnp.exp2(ms[h] - m_new) * acc_ref[h] + jnp.dot(vt, p, preferred_element_type=F32)
            out.append(m_new)
        return tuple(out)

    init = tuple(jnp.full((1, tq), NEG_BIG, F32) for _ in range(hps))
    ms = lax.fori_loop(0, i, lambda j, c: tile(j, c, False), init)
    tile(i, ms, True)
    for h in range(hps):
        acc = acc_ref[h]
        o_ref[0, :, h * V_DIM:(h + 1) * V_DIM] = (acc[:V_DIM] / acc[V_DIM:V_DIM + 1]).T.astype(BF16)


def _attention(qt, kc, vt, tq, hps):
    batch, _, _, seq = qt.shape
    return pl.pallas_call(
        functools.partial(_attn_kernel, tq=tq),
        grid=(batch, N_HEADS // hps, seq // tq),
        in_specs=[pl.BlockSpec((1, hps, QK_PAD, tq), lambda b, h, i: (b, h, 0, i)),
                  pl.BlockSpec((1, hps, seq, QK_PAD), lambda b, h, i: (b, h, 0, 0), pipeline_mode=pl.Buffered(1)),
                  pl.BlockSpec((1, hps, VT_PAD, seq), lambda b, h, i: (b, h, 0, 0), pipeline_mode=pl.Buffered(1))],
        out_specs=pl.BlockSpec((1, tq, hps * V_DIM), lambda b, h, i: (b, i, h)),
        out_shape=jax.ShapeDtypeStruct((batch, seq, N_HEADS * V_DIM), BF16),
        scratch_shapes=[pltpu.VMEM((hps, VT_PAD, tq), F32)],
        compiler_params=_cparams("arbitrary", "arbitrary", "arbitrary"),
        name="attention",
    )(qt, kc, vt)


def _oproj_kernel(o_ref, x_ref, wo_ref, fn_ref, wr_ref, br_ref, x3_ref, hp_ref, lg_ref):
    x3 = x_ref[...] + jnp.dot(o_ref[...], wo_ref[...], preferred_element_type=F32)
    x3_ref[...] = x3
    _ffn_prologue(x3, fn_ref[...], wr_ref[...], br_ref[...], hp_ref, lg_ref)


def _oproj(o2d, x2d, wo, fn, wr, br, ts):
    t, d = x2d.shape
    full = lambda a: pl.BlockSpec(a.shape, lambda i: (0,) * a.ndim)
    return pl.pallas_call(
        _oproj_kernel,
        grid=(t // ts,),
        in_specs=[pl.BlockSpec((ts, o2d.shape[1]), lambda i: (i, 0)),
                  pl.BlockSpec((ts, d), lambda i: (i, 0)),
                  full(wo), full(fn), full(wr), full(br)],
        out_specs=[pl.BlockSpec((ts, d), lambda i: (i, 0)),
                   pl.BlockSpec((ts, d // 2), lambda i: (i, 0)),
                   pl.BlockSpec((ts, LANES), lambda i: (i, 0))],
        out_shape=[jax.ShapeDtypeStruct((t, d), F32),
                   jax.ShapeDtypeStruct((t, d // 2), jnp.uint32),
                   jax.ShapeDtypeStruct((t, LANES), F32)],
        compiler_params=_cparams("arbitrary"),
        name="out_proj",
    )(o2d, x2d, wo, fn, wr, br)


def _router_params(rg_w, rg_b, re_w, re_b):
    d = rg_w.shape[0]
    used = N_GROUPS + N_EXPERTS
    wr = jnp.concatenate([re_w, rg_w, jnp.zeros((d, LANES - used), F32)], axis=1)
    br = jnp.concatenate([re_b, rg_b, jnp.zeros((LANES - used,), F32)])[None, :]
    return wr, br


def _with_rotate_half(w):
    half = w.shape[-1] // 2
    return jnp.concatenate([w, -w[..., half:], w[..., :half]], axis=-1)


def _rope_table(seq):
    half = QK_ROPE // 2
    inv = ROPE_THETA ** (-jnp.arange(half, dtype=F32) / half)
    ang = jnp.arange(seq, dtype=F32)[:, None] * inv[None, :]
    cos, sin = jnp.cos(ang), jnp.sin(ang)
    return jnp.concatenate([cos, cos, sin, sin], axis=1)


def kernel(x, pool_norm, pool_w, pool_b, pool_scale, kv_in_norm, w_dkv, kv_norm, w_uk, w_uv, attn_norm, wq_down, q_norm, wq_up, wo, ffn_norm, router_group_w, router_group_b, router_expert_w, router_expert_b, w_gate, w_up, w_down, final_norm):
    batch, seq, d = x.shape
    t = batch * seq
    depth = ffn_norm.shape[0]
    n_a = pool_norm.shape[0]
    ts = min(256, seq)
    tq = min(512, seq)
    row = lambda a: a.reshape(1, -1)

    cs = _rope_table(seq)
    routers = [_router_params(router_group_w[l], router_group_b[l], router_expert_w[l], router_expert_b[l])
               for l in range(depth)]
    wdkv = jnp.concatenate([w_dkv[:, :KV_RANK], _with_rotate_half(w_dkv[:, KV_RANK:])], axis=1).astype(BF16)
    wuk, wuvt = w_uk.astype(BF16), w_uv.T.astype(BF16)
    cst = cs.T

    x2d = x.reshape(t, d)
    kc = vv = None
    for l in range(depth):
        wr, br = routers[l]
        if l < n_a:
            x2d, hp, logits = _pool_layer(x2d, seq, row(pool_norm[l]), pool_w[l].astype(BF16), pool_b[l][:, None, :],
                                          row(pool_scale[l]), row(ffn_norm[l]), wr, br, ts)
        else:
            j = l - n_a
            wqu = wq_up[j].reshape(-1, N_HEADS, QK_NOPE + QK_ROPE)
            wqu = jnp.concatenate([wqu[:, :, :QK_NOPE].reshape(-1, N_HEADS * QK_NOPE),
                                   _with_rotate_half(wqu[:, :, QK_NOPE:]).reshape(-1, N_HEADS * LANES)],
                                  axis=1).T.astype(BF16)
            k_new, v_new, qt = _qkv(x2d, batch, seq, row(kv_in_norm), row(attn_norm[j]), wdkv, row(kv_norm),
                                    wuk, wuvt, wq_down[j].astype(BF16), row(q_norm[j]), wqu, cs, cst, ts)
            if kc is None:
                kc, vv = k_new, v_new
            o = _attention(qt, kc, vv, tq, 4)
            x2d, hp, logits = _oproj(o.reshape(t, -1), x2d, wo[j].astype(BF16), row(ffn_norm[l]), wr, br, ts)
        x2d = _moe(x2d, hp, logits, w_gate, w_up, w_down, l, row(final_norm), l == depth - 1, ts)
    return x2d.reshape(batch, seq, d)
```

```python
import functools
import math

import jax
import jax.numpy as jnp
from jax import lax
from jax.experimental import pallas as pl
from jax.experimental.pallas import tpu as pltpu

EPS = 1e-6
CHUNK = 64
POOL_WINDOWS = (2, 4, 8, 16)
N_HEADS = 8
QK_NOPE = 128
QK_ROPE = 64
V_DIM = 128
KV_RANK = 256
ROPE_THETA = 10000.0
ATTN_SCALE = 1.0 / math.sqrt(QK_NOPE + QK_ROPE)
N_GROUPS = 4
EXPERTS_PER_GROUP = 8
N_EXPERTS = N_GROUPS * EXPERTS_PER_GROUP
ROW_BLOCK = 256

LANES = 128
SUBLANES = 8
HALO = 16
QK_PAD = 256
VT_PAD = V_DIM + 16
LOG2E = math.log2(math.e)
VMEM_LIMIT = 56 * 1024 * 1024
NEG_BIG = -1e30

F32 = jnp.float32
BF16 = jnp.bfloat16


def _cparams(*sem):
    return pltpu.CompilerParams(dimension_semantics=sem, vmem_limit_bytes=VMEM_LIMIT)


def _rms(x, g):
    return x * lax.rsqrt(jnp.mean(x * x, axis=-1, keepdims=True) + EPS) * g


def _store_row_tiles(ref, val, base=0):
    n = val.shape[0]
    for c in range(SUBLANES):
        ref[pl.ds(base + c, n, stride=SUBLANES), :] = val[:, c * LANES:(c + 1) * LANES]


def _load_row_tiles(ref, n, base=0):
    return jnp.concatenate([ref[pl.ds(base + c, n, stride=SUBLANES), :] for c in range(SUBLANES)], axis=1)


def _ffn_prologue(x, fn, wr, br, hp_ref, lg_ref):
    h = _rms(x, fn)
    _store_row_tiles(hp_ref, h)
    lg_ref[...] = jnp.dot(h, wr, precision=lax.Precision.HIGHEST,
                          preferred_element_type=F32) + br


def _pool_kernel(x_ref, pn_ref, pw_ref, pb_ref, ps_ref, fn_ref, wr_ref, br_ref,
                 x1_ref, hp_ref, lg_ref, buf_ref, *, ts, tiles_per_seq):
    i = pl.program_id(0)
    seq_tile = i % tiles_per_seq
    x = x_ref[...]
    h = _rms(x, pn_ref[...])

    @pl.when(seq_tile == 0)
    def _():
        buf_ref[0:HALO, :] = jnp.zeros((HALO, x.shape[1]), F32)

    buf_ref[HALO:, :] = h
    pos = lax.broadcasted_iota(jnp.int32, (ts, 1), 0) + seq_tile * ts
    gd = x.shape[1] // len(POOL_WINDOWS)
    for g, win in enumerate(POOL_WINDOWS):
        cols = slice(g * gd, (g + 1) * gd)
        s = buf_ref[:, cols]
        k = 1
        while k < win:
            s = s + pltpu.roll(s, k, axis=0)
            k *= 2
        cnt = jnp.minimum(pos + 1, win).astype(F32)
        pooled = s[HALO:, :] / cnt - h[:, cols]
        y = jnp.dot(pooled.astype(BF16), pw_ref[g], preferred_element_type=F32) + pb_ref[g]
        x1_ref[:, cols] = x[:, cols] + y * ps_ref[:, cols]
    buf_ref[0:HALO, :] = h[ts - HALO:, :]
    _ffn_prologue(x1_ref[...], fn_ref[...], wr_ref[...], br_ref[...], hp_ref, lg_ref)


def _pool_layer(x2d, seq, pn, pw, pb, ps, fn, wr, br, ts):
    t, d = x2d.shape
    full = lambda *shape: pl.BlockSpec(shape, lambda i: (0,) * len(shape))
    return pl.pallas_call(
        functools.partial(_pool_kernel, ts=ts, tiles_per_seq=seq // ts),
        grid=(t // ts,),
        in_specs=[pl.BlockSpec((ts, d), lambda i: (i, 0)),
                  full(1, d), full(*pw.shape), full(*pb.shape), full(1, d), full(1, d),
                  full(d, LANES), full(1, LANES)],
        out_specs=[pl.BlockSpec((ts, d), lambda i: (i, 0)),
                   pl.BlockSpec((ts * SUBLANES, LANES), lambda i: (i, 0)),
                   pl.BlockSpec((ts, LANES), lambda i: (i, 0))],
        out_shape=[jax.ShapeDtypeStruct((t, d), F32),
                   jax.ShapeDtypeStruct((t * SUBLANES, LANES), F32),
                   jax.ShapeDtypeStruct((t, LANES), F32)],
        scratch_shapes=[pltpu.VMEM((HALO + ts, d), F32)],
        compiler_params=_cparams("arbitrary"),
        name="pool_layer",
    )(x2d, pn, pw, pb, ps, fn, wr, br)


def _rows8(vals, width):
    row8 = lax.broadcasted_iota(jnp.int32, (8, width), 0)
    out = jnp.zeros((8, width), F32)
    for c, v in enumerate(vals):
        out = jnp.where(row8 == c, v, out)
    return out


def _route_kernel(lg_ref, gate_ref, plan_ref, blk_ref, st_ref, carry_ref, *, tr, nt):
    i = pl.program_id(0)

    @pl.when(i == 0)
    def _():
        carry_ref[...] = jnp.zeros_like(carry_ref)

    lgt = lg_ref[...].T
    neg = jnp.float32(-jnp.inf)
    row8 = lax.broadcasted_iota(jnp.int32, (8, tr), 0)
    row = lax.broadcasted_iota(jnp.int32, (N_EXPERTS, tr), 0)
    first = lambda hit, idx, n: jnp.min(jnp.where(hit, idx, n), axis=0, keepdims=True)

    gl = jnp.where(row8 < N_GROUPS, lgt[N_EXPERTS:N_EXPERTS + 8], neg)
    gmax = jnp.max(gl, axis=0, keepdims=True)
    gidx = first(gl == gmax, row8, 8)
    gprob = 1.0 / jnp.sum(jnp.exp(gl - gmax), axis=0, keepdims=True)

    el = jnp.where(row // EXPERTS_PER_GROUP == gidx, lgt[0:N_EXPERTS], neg)
    m1 = jnp.max(el, axis=0, keepdims=True)
    e0 = first(el == m1, row, N_EXPERTS)
    el2 = jnp.where(row == e0, neg, el)
    m2 = jnp.max(el2, axis=0, keepdims=True)
    e1 = first(el2 == m2, row, N_EXPERTS)
    r = jnp.exp(m2 - m1)
    g0 = gprob / (1.0 + r)
    g1 = gprob * r / (1.0 + r)

    oh0 = jnp.where(row == e0, 1.0, 0.0)
    oh1 = jnp.where(row == e1, 1.0, 0.0)
    oh = oh0 + oh1
    rr = lax.broadcasted_iota(jnp.int32, (tr, tr), 0)
    cc = lax.broadcasted_iota(jnp.int32, (tr, tr), 1)
    earlier = jnp.where(rr < cc, 1.0, 0.0).astype(BF16)
    carry = carry_ref[...]
    before = (jnp.dot(oh.astype(BF16), earlier, preferred_element_type=F32)
              + jnp.tile(carry, (1, tr // LANES)))
    rank0 = jnp.sum(before * oh0, axis=0, keepdims=True)
    rank1 = jnp.sum(before * oh1, axis=0, keepdims=True)
    carry_ref[...] = carry + jnp.sum(oh, axis=1, keepdims=True)

    st_ref[:, pl.ds(pl.multiple_of(i * tr, tr), tr)] = _rows8(
        (e0.astype(F32), e1.astype(F32), rank0, rank1), tr)
    gate_ref[...] = jnp.concatenate([_rows8((g0, g1), tr), jnp.zeros((LANES - 8, tr), F32)], axis=0).T

    @pl.when(i == nt - 1)
    def _():
        cnt = carry_ref[...]
        nblk = jnp.floor((cnt + (ROW_BLOCK - 1)) * (1.0 / ROW_BLOCK))
        er = lax.broadcasted_iota(jnp.int32, cnt.shape, 0)
        ec = lax.broadcasted_iota(jnp.int32, cnt.shape, 1)
        nblk_row = jnp.sum(jnp.where(er == ec, nblk, 0.0), axis=0, keepdims=True)
        pstart = jnp.sum(jnp.where(ec < er, nblk_row, 0.0), axis=1, keepdims=True)
        pend = pstart + nblk[:, 0:1]
        bidx = lax.broadcasted_iota(jnp.int32, (N_EXPERTS, blk_ref.shape[1]), 1).astype(F32)
        block_e = jnp.minimum(jnp.sum(jnp.where(pend <= bidx, 1.0, 0.0), axis=0, keepdims=True),
                              N_EXPERTS - 1.0)
        total = jnp.sum(nblk_row, axis=1, keepdims=True)
        blk_ref[...] = _rows8((block_e, jnp.broadcast_to(total, block_e.shape)), blk_ref.shape[1])
        rowf = row.astype(F32)

        def dests(j, c):
            sl = pl.ds(pl.multiple_of(j * tr, tr), tr)
            st = st_ref[:, sl]
            base = lambda e: ROW_BLOCK * jnp.sum(jnp.where(rowf == e, pstart, 0.0), axis=0, keepdims=True)
            plan_ref[:, sl] = _rows8((base(st[0:1]) + st[2:3], base(st[1:2]) + st[3:4]), tr)
            return c

        lax.fori_loop(0, nt, dests, 0)


def _route(logits, tr, n_blocks):
    t = logits.shape[0]
    nt = t // tr
    nbp = -(-n_blocks // LANES) * LANES
    return pl.pallas_call(
        functools.partial(_route_kernel, tr=tr, nt=nt),
        grid=(nt,),
        in_specs=[pl.BlockSpec((tr, LANES), lambda i: (i, 0))],
        out_specs=[pl.BlockSpec((tr, LANES), lambda i: (i, 0)),
                   pl.BlockSpec((8, t), lambda i: (0, 0)),
                   pl.BlockSpec((8, nbp), lambda i: (0, 0))],
        out_shape=[jax.ShapeDtypeStruct((t, LANES), F32),
                   jax.ShapeDtypeStruct((8, t), F32),
                   jax.ShapeDtypeStruct((8, nbp), F32)],
        scratch_shapes=[pltpu.VMEM((8, t), F32), pltpu.VMEM((N_EXPERTS, LANES), F32)],
        compiler_params=_cparams("arbitrary"),
        name="route",
    )(logits)


def _row_tile(ref, r):
    return ref.at[pl.ds(pl.multiple_of(r * SUBLANES, SUBLANES), SUBLANES)]


def _dispatch_kernel(d0_ref, d1_ref, h_ref, xs_in_ref, xs_ref, sem, *, ts):
    del xs_in_ref

    def issue(r, c):
        for d_ref in (d0_ref, d1_ref):
            pltpu.make_async_copy(_row_tile(h_ref, r), _row_tile(xs_ref, d_ref[0, 0, r]), sem).start()
        return c

    lax.fori_loop(0, ts, issue, 0)
    for k in range(2):
        pltpu.make_async_copy(h_ref, xs_ref.at[pl.ds(0, ts * SUBLANES)], sem).wait()


def _dest_spec(ts):
    return pl.BlockSpec((1, 1, ts), lambda i: (i, 0, 0), memory_space=pltpu.SMEM)


def _dispatch(hp, d0, d1, n_rows, ts):
    xs0 = jnp.zeros((n_rows * SUBLANES, LANES), F32)
    return pl.pallas_call(
        functools.partial(_dispatch_kernel, ts=ts),
        grid=(hp.shape[0] // (ts * SUBLANES),),
        in_specs=[_dest_spec(ts), _dest_spec(ts),
                  pl.BlockSpec((ts * SUBLANES, LANES), lambda i: (i, 0)),
                  pl.BlockSpec(memory_space=pl.ANY)],
        out_specs=pl.BlockSpec(memory_space=pl.ANY),
        out_shape=jax.ShapeDtypeStruct(xs0.shape, F32),
        scratch_shapes=[pltpu.SemaphoreType.DMA(())],
        input_output_aliases={3: 0},
        compiler_params=_cparams("arbitrary"),
        name="dispatch",
    )(d0, d1, hp, xs0)


def _expert_kernel(be_ref, nu_ref, xs_ref, wg_ref, wu_ref, wd_ref, ys_ref, wg_s, wu_s, wd_s):
    b = pl.program_id(0)

    @pl.when(b < nu_ref[0])
    def _():
        prev = be_ref[jnp.maximum(b - 1, 0)]

        @pl.when((b == 0) | (be_ref[b] != prev))
        def _():
            wg_s[...] = wg_ref[0, 0].astype(BF16)
            wu_s[...] = wu_ref[0, 0].astype(BF16)
            wd_s[...] = wd_ref[0, 0].astype(BF16)

        x = _load_row_tiles(xs_ref, ROW_BLOCK).astype(BF16)
        mm = lambda a, w: jnp.dot(a, w, preferred_element_type=F32)
        gt = mm(x, wg_s[...])
        up = mm(x, wu_s[...])
        act = gt * (1.0 / (1.0 + jnp.exp(-gt))) * up
        _store_row_tiles(ys_ref, mm(act.astype(BF16), wd_s[...]))

    @pl.when(b >= nu_ref[0])
    def _():
        ys_ref[...] = jnp.zeros_like(ys_ref)


def _experts(xs, block_e, nb_used, w_gate, w_up, w_down, layer):
    n_blocks = xs.shape[0] // (ROW_BLOCK * SUBLANES)
    _, _, d, de = w_gate.shape
    blk = lambda b, be, nu: (jnp.minimum(b, nu[0] - 1), 0)
    wsel = lambda b, be, nu: (layer, be[jnp.minimum(b, nu[0] - 1)], 0, 0)
    return pl.pallas_call(
        _expert_kernel,
        grid_spec=pltpu.PrefetchScalarGridSpec(
            num_scalar_prefetch=2,
            grid=(n_blocks,),
            in_specs=[pl.BlockSpec((ROW_BLOCK * SUBLANES, LANES), blk),
                      pl.BlockSpec((1, 1, d, de), wsel),
                      pl.BlockSpec((1, 1, d, de), wsel),
                      pl.BlockSpec((1, 1, de, d), wsel)],
            out_specs=pl.BlockSpec((ROW_BLOCK * SUBLANES, LANES), lambda b, be, nu: (b, 0)),
            scratch_shapes=[pltpu.VMEM((d, de), BF16), pltpu.VMEM((d, de), BF16),
                            pltpu.VMEM((de, d), BF16)]),
        out_shape=jax.ShapeDtypeStruct(xs.shape, F32),
        compiler_params=_cparams("arbitrary"),
        name="experts",
    )(block_e, nb_used, xs, w_gate, w_up, w_down)


def _combine_kernel(d0_ref, d1_ref, x_ref, gate_ref, ys_ref, fin_ref, out_ref, ybuf, sem, *, ts, final):
    def issue(r, c):
        for k, d_ref in enumerate((d0_ref, d1_ref)):
            pltpu.make_async_copy(_row_tile(ys_ref, d_ref[0, 0, r]), _row_tile(ybuf, k * ts + r), sem).start()
        return c

    lax.fori_loop(0, ts, issue, 0)
    pltpu.make_async_copy(ys_ref.at[pl.ds(0, 2 * ts * SUBLANES)], ybuf, sem).wait()
    gate = gate_ref[...]
    out = (x_ref[...] + gate[:, 0:1] * _load_row_tiles(ybuf, ts)
           + gate[:, 1:2] * _load_row_tiles(ybuf, ts, ts * SUBLANES))
    if final:
        out = _rms(out, fin_ref[...])
    out_ref[...] = out


def _combine(x2d, gate, d0, d1, ys, fin, ts, final):
    t, d = x2d.shape
    return pl.pallas_call(
        functools.partial(_combine_kernel, ts=ts, final=final),
        grid=(t // ts,),
        in_specs=[_dest_spec(ts), _dest_spec(ts),
                  pl.BlockSpec((ts, d), lambda i: (i, 0)),
                  pl.BlockSpec((ts, LANES), lambda i: (i, 0)),
                  pl.BlockSpec(memory_space=pl.ANY),
                  pl.BlockSpec((1, d), lambda i: (0, 0))],
        out_specs=pl.BlockSpec((ts, d), lambda i: (i, 0)),
        out_shape=jax.ShapeDtypeStruct((t, d), F32),
        scratch_shapes=[pltpu.VMEM((2 * ts * SUBLANES, LANES), F32), pltpu.SemaphoreType.DMA(())],
        compiler_params=_cparams("arbitrary"),
        name="combine",
    )(d0, d1, x2d, gate, ys, fin)


def _moe(x2d, hp, logits, w_gate, w_up, w_down, layer, fin, final, ts):
    t = x2d.shape[0]
    n_blocks = (2 * t) // ROW_BLOCK + N_EXPERTS
    gate, plan, blk = _route(logits, min(512, t), n_blocks)
    d0 = plan[0].astype(jnp.int32).reshape(t // ts, 1, ts)
    d1 = plan[1].astype(jnp.int32).reshape(t // ts, 1, ts)
    block_e = blk[0, :n_blocks].astype(jnp.int32)
    nb_used = blk[1, 0:1].astype(jnp.int32)
    xs = _dispatch(hp, d0, d1, n_blocks * ROW_BLOCK, ts)
    ys = _experts(xs, block_e, nb_used, w_gate, w_up, w_down, layer)
    return _combine(x2d, gate, d0, d1, ys, fin, ts, final)


def _rope_pairs(t, cs):
    p = t * cs
    return p + pltpu.roll(p, QK_ROPE, axis=1)


def _qkv_kernel(x_ref, kn_ref, an_ref, wdkv_ref, kvn_ref, wuk_ref, wuvt_ref, wqd_ref, qn_ref, wqut_ref,
                cs_ref, cst_ref, k_ref, vt_ref, qt_ref):
    x = x_ref[...]
    xn = x * lax.rsqrt(jnp.mean(x * x, axis=-1, keepdims=True) + EPS)
    cs = cs_ref[...]
    mm = lambda a, w: jnp.dot(a.astype(BF16), w, preferred_element_type=F32)
    lane = lax.broadcasted_iota(jnp.int32, cs.shape, 1)

    ckv = mm(xn * kn_ref[...], wdkv_ref[...])
    c_kv = _rms(ckv[:, :KV_RANK], kvn_ref[...])
    k_rope = jnp.where(lane < QK_ROPE, _rope_pairs(ckv[:, KV_RANK:], cs), 0.0).astype(BF16)
    k_nope = mm(c_kv, wuk_ref[...]).astype(BF16)
    for h in range(N_HEADS):
        k_ref[0, h] = jnp.concatenate([k_nope[:, h * QK_NOPE:(h + 1) * QK_NOPE], k_rope], axis=-1)

    mmt = lambda wt, a: lax.dot_general(wt, a.astype(BF16), (((1,), (1,)), ((), ())),
                                        preferred_element_type=F32)
    ts = x.shape[0]
    vt = mmt(wuvt_ref[...], c_kv).astype(BF16)
    one_row = jnp.where(lax.broadcasted_iota(jnp.int32, (VT_PAD - V_DIM, ts), 0) == 0, 1.0, 0.0).astype(BF16)
    cq = _rms(mm(xn * an_ref[...], wqd_ref[...]), qn_ref[...])
    qt = mmt(wqut_ref[...], cq) * (ATTN_SCALE * LOG2E)
    cst = cst_ref[...]
    nope_w = N_HEADS * QK_NOPE
    zpad = jnp.zeros((QK_PAD - QK_NOPE - QK_ROPE, ts), F32)
    for h in range(N_HEADS):
        vt_ref[0, h] = jnp.concatenate([vt[h * V_DIM:(h + 1) * V_DIM], one_row], axis=0)
        rp = qt[nope_w + h * LANES:nope_w + (h + 1) * LANES] * cst
        rope = rp[:QK_ROPE] + rp[QK_ROPE:]
        qt_ref[0, h] = jnp.concatenate([qt[h * QK_NOPE:(h + 1) * QK_NOPE], rope, zpad], axis=0).astype(BF16)


def _qkv(x2d, batch, seq, kn, an, wdkv, kvn, wuk, wuvt, wqd, qn, wqut, cs, cst, ts):
    t, d = x2d.shape
    tps = seq // ts
    full = lambda a: pl.BlockSpec(a.shape, lambda i: (0,) * a.ndim)
    tspec = lambda rows: pl.BlockSpec((1, N_HEADS, rows, ts), lambda i: (i // tps, 0, 0, i % tps))
    return pl.pallas_call(
        _qkv_kernel,
        grid=(t // ts,),
        in_specs=[pl.BlockSpec((ts, d), lambda i: (i, 0)),
                  full(kn), full(an), full(wdkv), full(kvn), full(wuk), full(wuvt), full(wqd), full(qn),
                  full(wqut), pl.BlockSpec((ts, LANES), lambda i: (i % tps, 0)),
                  pl.BlockSpec((LANES, ts), lambda i: (0, i % tps))],
        out_specs=[pl.BlockSpec((1, N_HEADS, ts, QK_PAD), lambda i: (i // tps, 0, i % tps, 0)),
                   tspec(VT_PAD), tspec(QK_PAD)],
        out_shape=[jax.ShapeDtypeStruct((batch, N_HEADS, seq, QK_PAD), BF16),
                   jax.ShapeDtypeStruct((batch, N_HEADS, VT_PAD, seq), BF16),
                   jax.ShapeDtypeStruct((batch, N_HEADS, QK_PAD, seq), BF16)],
        compiler_params=_cparams("arbitrary"),
        name="qkv_proj",
    )(x2d, kn, an, wdkv, kvn, wuk, wuvt, wqd, qn, wqut, cs, cst)


def _attn_kernel(qt_ref, k_ref, vt_ref, o_ref, acc_ref, *, tq):
    i = pl.program_id(2)
    hps = qt_ref.shape[1]
    acc_ref[...] = jnp.zeros_like(acc_ref)

    def tile(j, ms, masked):
        start = pl.multiple_of(j * tq, tq)
        scores = [jnp.dot(k_ref[0, h, pl.ds(start, tq), :], qt_ref[0, h], preferred_element_type=F32)
                  for h in range(hps)]
        out = []
        for h in range(hps):
            vt = vt_ref[0, h, :, pl.ds(start, tq)]
            s = scores[h]
            if masked:
                kc = lax.broadcasted_iota(jnp.int32, s.shape, 0) // CHUNK
                qc = lax.broadcasted_iota(jnp.int32, s.shape, 1) // CHUNK
                s = jnp.where(kc <= qc, s, NEG_BIG)
            m_new = jnp.maximum(ms[h], jnp.max(s, axis=0, keepdims=True))
            p = jnp.exp2((s - m_new).astype(BF16))
            acc_ref[h] = jnp.exp2(ms[h] - m_new) * acc_ref[h] + jnp.dot(vt, p, preferred_element_type=F32)
            out.append(m_new)
        return tuple(out)

    init = tuple(jnp.full((1, tq), NEG_BIG, F32) for _ in range(hps))
    ms = lax.fori_loop(0, i, lambda j, c: tile(j, c, False), init)
    tile(i, ms, True)
    for h in range(hps):
        acc = acc_ref[h]
        o_ref[0, :, h * V_DIM:(h + 1) * V_DIM] = (acc[:V_DIM] / acc[V_DIM:V_DIM + 1]).T.astype(BF16)


def _attention(qt, kc, vt, tq, hps):
    batch, _, _, seq = qt.shape
    return pl.pallas_call(
        functools.partial(_attn_kernel, tq=tq),
        grid=(batch, N_HEADS // hps, seq // tq),
        in_specs=[pl.BlockSpec((1, hps, QK_PAD, tq), lambda b, h, i: (b, h, 0, i)),
                  pl.BlockSpec((1, hps, seq, QK_PAD), lambda b, h, i: (b, h, 0, 0), pipeline_mode=pl.Buffered(1)),
                  pl.BlockSpec((1, hps, VT_PAD, seq), lambda b, h, i: (b, h, 0, 0), pipeline_mode=pl.Buffered(1))],
        out_specs=pl.BlockSpec((1, tq, hps * V_DIM), lambda b, h, i: (b, i, h)),
        out_shape=jax.ShapeDtypeStruct((batch, seq, N_HEADS * V_DIM), BF16),
        scratch_shapes=[pltpu.VMEM((hps, VT_PAD, tq), F32)],
        compiler_params=_cparams("arbitrary", "arbitrary", "arbitrary"),
        name="attention",
    )(qt, kc, vt)


def _oproj_kernel(o_ref, x_ref, wo_ref, fn_ref, wr_ref, br_ref, x3_ref, hp_ref, lg_ref):
    x3 = x_ref[...] + jnp.dot(o_ref[...], wo_ref[...], preferred_element_type=F32)
    x3_ref[...] = x3
    _ffn_prologue(x3, fn_ref[...], wr_ref[...], br_ref[...], hp_ref, lg_ref)


def _oproj(o2d, x2d, wo, fn, wr, br, ts):
    t, d = x2d.shape
    full = lambda a: pl.BlockSpec(a.shape, lambda i: (0,) * a.ndim)
    return pl.pallas_call(
        _oproj_kernel,
        grid=(t // ts,),
        in_specs=[pl.BlockSpec((ts, o2d.shape[1]), lambda i: (i, 0)),
                  pl.BlockSpec((ts, d), lambda i: (i, 0)),
                  full(wo), full(fn), full(wr), full(br)],
        out_specs=[pl.BlockSpec((ts, d), lambda i: (i, 0)),
                   pl.BlockSpec((ts * SUBLANES, LANES), lambda i: (i, 0)),
                   pl.BlockSpec((ts, LANES), lambda i: (i, 0))],
        out_shape=[jax.ShapeDtypeStruct((t, d), F32),
                   jax.ShapeDtypeStruct((t * SUBLANES, LANES), F32),
                   jax.ShapeDtypeStruct((t, LANES), F32)],
        compiler_params=_cparams("arbitrary"),
        name="out_proj",
    )(o2d, x2d, wo, fn, wr, br)


def _router_params(rg_w, rg_b, re_w, re_b):
    d = rg_w.shape[0]
    used = N_GROUPS + N_EXPERTS
    wr = jnp.concatenate([re_w, rg_w, jnp.zeros((d, LANES - used), F32)], axis=1)
    br = jnp.concatenate([re_b, rg_b, jnp.zeros((LANES - used,), F32)])[None, :]
    return wr, br


def _with_rotate_half(w):
    half = w.shape[-1] // 2
    return jnp.concatenate([w, -w[..., half:], w[..., :half]], axis=-1)


def _rope_table(seq):
    half = QK_ROPE // 2
    inv = ROPE_THETA ** (-jnp.arange(half, dtype=F32) / half)
    ang = jnp.arange(seq, dtype=F32)[:, None] * inv[None, :]
    cos, sin = jnp.cos(ang), jnp.sin(ang)
    return jnp.concatenate([cos, cos, sin, sin], axis=1)


def kernel(x, pool_norm, pool_w, pool_b, pool_scale, kv_in_norm, w_dkv, kv_norm, w_uk, w_uv, attn_norm, wq_down, q_norm, wq_up, wo, ffn_norm, router_group_w, router_group_b, router_expert_w, router_expert_b, w_gate, w_up, w_down, final_norm):
    batch, seq, d = x.shape
    t = batch * seq
    depth = ffn_norm.shape[0]
    n_a = pool_norm.shape[0]
    ts = min(256, seq)
    tq = min(512, seq)
    row = lambda a: a.reshape(1, -1)

    cs = _rope_table(seq)
    routers = [_router_params(router_group_w[l], router_group_b[l], router_expert_w[l], router_expert_b[l])
               for l in range(depth)]
    wdkv = jnp.concatenate([w_dkv[:, :KV_RANK], _with_rotate_half(w_dkv[:, KV_RANK:])], axis=1).astype(BF16)
    wuk, wuvt = w_uk.astype(BF16), w_uv.T.astype(BF16)
    cst = cs.T

    x2d = x.reshape(t, d)
    kc = vv = None
    for l in range(depth):
        wr, br = routers[l]
        if l < n_a:
            x2d, hp, logits = _pool_layer(x2d, seq, row(pool_norm[l]), pool_w[l].astype(BF16), pool_b[l][:, None, :],
                                          row(pool_scale[l]), row(ffn_norm[l]), wr, br, ts)
        else:
            j = l - n_a
            wqu = wq_up[j].reshape(-1, N_HEADS, QK_NOPE + QK_ROPE)
            wqu = jnp.concatenate([wqu[:, :, :QK_NOPE].reshape(-1, N_HEADS * QK_NOPE),
                                   _with_rotate_half(wqu[:, :, QK_NOPE:]).reshape(-1, N_HEADS * LANES)],
                                  axis=1).T.astype(BF16)
            k_new, v_new, qt = _qkv(x2d, batch, seq, row(kv_in_norm), row(attn_norm[j]), wdkv, row(kv_norm),
                                    wuk, wuvt, wq_down[j].astype(BF16), row(q_norm[j]), wqu, cs, cst, ts)
            if kc is None:
                kc, vv = k_new, v_new
            o = _attention(qt, kc, vv, tq, 4)
            x2d, hp, logits = _oproj(o.reshape(t, -1), x2d, wo[j].astype(BF16), row(ffn_norm[l]), wr, br, ts)
        x2d = _moe(x2d, hp, logits, w_gate, w_up, w_down, l, row(final_norm), l == depth - 1, ts)
    return x2d.reshape(batch, seq, d)
```

```python
import functools
import math

import jax
import jax.numpy as jnp
from jax import lax
from jax.experimental import pallas as pl
from jax.experimental.pallas import tpu as pltpu

EPS = 1e-6
CHUNK = 64
POOL_WINDOWS = (2, 4, 8, 16)
N_HEADS = 8
QK_NOPE = 128
QK_ROPE = 64
V_DIM = 128
KV_RANK = 256
ROPE_THETA = 10000.0
ATTN_SCALE = 1.0 / math.sqrt(QK_NOPE + QK_ROPE)
N_GROUPS = 4
EXPERTS_PER_GROUP = 8
N_EXPERTS = N_GROUPS * EXPERTS_PER_GROUP
ROW_BLOCK = 256

LANES = 128
SUBLANES = 8
HALO = 16
QK_PAD = 256
VT_PAD = V_DIM + 16
LOG2E = math.log2(math.e)
VMEM_LIMIT = 56 * 1024 * 1024
NEG_BIG = -1e30
ISSUE_UNROLL = 8

F32 = jnp.float32
BF16 = jnp.bfloat16


def _cparams(*sem):
    return pltpu.CompilerParams(dimension_semantics=sem, vmem_limit_bytes=VMEM_LIMIT)


def _rms(x, g):
    return x * lax.rsqrt(jnp.mean(x * x, axis=-1, keepdims=True) + EPS) * g


def _store_row_tiles(ref, val, chunk0=0):
    n = val.shape[0]
    for c in range(val.shape[1] // LANES):
        ref[pl.ds(chunk0 + c, n, stride=SUBLANES), :] = val[:, c * LANES:(c + 1) * LANES]


def _load_row_tiles(ref, n, base=0):
    return jnp.concatenate([ref[pl.ds(base + c, n, stride=SUBLANES), :] for c in range(SUBLANES)], axis=1)


def _ffn_prologue(x, fn, wr, br, hp_ref, lg_ref):
    h = _rms(x, fn)
    _store_row_tiles(hp_ref, h)
    lg_ref[...] = jnp.dot(h, wr, precision=lax.Precision.HIGHEST,
                          preferred_element_type=F32) + br


def _pool_kernel(x_ref, pn_ref, pw_ref, pb_ref, ps_ref, fn_ref, wr_ref, br_ref,
                 x1_ref, hp_ref, lg_ref, buf_ref, *, ts, tiles_per_seq):
    i = pl.program_id(0)
    seq_tile = i % tiles_per_seq
    x = x_ref[...]
    h = _rms(x, pn_ref[...])

    @pl.when(seq_tile == 0)
    def _():
        buf_ref[0:HALO, :] = jnp.zeros((HALO, x.shape[1]), F32)

    buf_ref[HALO:, :] = h
    pos = lax.broadcasted_iota(jnp.int32, (ts, 1), 0) + seq_tile * ts
    gd = x.shape[1] // len(POOL_WINDOWS)
    for g, win in enumerate(POOL_WINDOWS):
        cols = slice(g * gd, (g + 1) * gd)
        s = buf_ref[:, cols]
        k = 1
        while k < win:
            s = s + pltpu.roll(s, k, axis=0)
            k *= 2
        cnt = jnp.minimum(pos + 1, win).astype(F32)
        pooled = s[HALO:, :] / cnt - h[:, cols]
        y = jnp.dot(pooled.astype(BF16), pw_ref[g], preferred_element_type=F32) + pb_ref[g]
        x1_ref[:, cols] = x[:, cols] + y * ps_ref[:, cols]
    buf_ref[0:HALO, :] = h[ts - HALO:, :]
    _ffn_prologue(x1_ref[...], fn_ref[...], wr_ref[...], br_ref[...], hp_ref, lg_ref)


def _pool_layer(x2d, seq, pn, pw, pb, ps, fn, wr, br, ts):
    t, d = x2d.shape
    full = lambda *shape: pl.BlockSpec(shape, lambda i: (0,) * len(shape))
    return pl.pallas_call(
        functools.partial(_pool_kernel, ts=ts, tiles_per_seq=seq // ts),
        grid=(t // ts,),
        in_specs=[pl.BlockSpec((ts, d), lambda i: (i, 0)),
                  full(1, d), full(*pw.shape), full(*pb.shape), full(1, d), full(1, d),
                  full(d, LANES), full(1, LANES)],
        out_specs=[pl.BlockSpec((ts, d), lambda i: (i, 0)),
                   pl.BlockSpec((ts * SUBLANES, LANES), lambda i: (i, 0)),
                   pl.BlockSpec((ts, LANES), lambda i: (i, 0))],
        out_shape=[jax.ShapeDtypeStruct((t, d), F32),
                   jax.ShapeDtypeStruct((t * SUBLANES, LANES), F32),
                   jax.ShapeDtypeStruct((t, LANES), F32)],
        scratch_shapes=[pltpu.VMEM((HALO + ts, d), F32)],
        compiler_params=_cparams("arbitrary"),
        name="pool_layer",
    )(x2d, pn, pw, pb, ps, fn, wr, br)


def _rows8(vals, width):
    row8 = lax.broadcasted_iota(jnp.int32, (8, width), 0)
    out = jnp.zeros((8, width), F32)
    for c, v in enumerate(vals):
        out = jnp.where(row8 == c, v, out)
    return out


def _route_kernel(lg_ref, gate_ref, plan_ref, blk_ref, st_ref, carry_ref, *, tr, nt):
    i = pl.program_id(0)

    @pl.when(i == 0)
    def _():
        carry_ref[...] = jnp.zeros_like(carry_ref)

    lgt = lg_ref[...].T
    neg = jnp.float32(-jnp.inf)
    row8 = lax.broadcasted_iota(jnp.int32, (8, tr), 0)
    row = lax.broadcasted_iota(jnp.int32, (N_EXPERTS, tr), 0)
    first = lambda hit, idx, n: jnp.min(jnp.where(hit, idx, n), axis=0, keepdims=True)

    gl = jnp.where(row8 < N_GROUPS, lgt[N_EXPERTS:N_EXPERTS + 8], neg)
    gmax = jnp.max(gl, axis=0, keepdims=True)
    gidx = first(gl == gmax, row8, 8)
    gprob = 1.0 / jnp.sum(jnp.exp(gl - gmax), axis=0, keepdims=True)

    el = jnp.where(row // EXPERTS_PER_GROUP == gidx, lgt[0:N_EXPERTS], neg)
    m1 = jnp.max(el, axis=0, keepdims=True)
    e0 = first(el == m1, row, N_EXPERTS)
    el2 = jnp.where(row == e0, neg, el)
    m2 = jnp.max(el2, axis=0, keepdims=True)
    e1 = first(el2 == m2, row, N_EXPERTS)
    r = jnp.exp(m2 - m1)
    g0 = gprob / (1.0 + r)
    g1 = gprob * r / (1.0 + r)

    oh0 = jnp.where(row == e0, 1.0, 0.0)
    oh1 = jnp.where(row == e1, 1.0, 0.0)
    oh = oh0 + oh1
    rr = lax.broadcasted_iota(jnp.int32, (tr, tr), 0)
    cc = lax.broadcasted_iota(jnp.int32, (tr, tr), 1)
    earlier = jnp.where(rr < cc, 1.0, 0.0).astype(BF16)
    carry = carry_ref[...]
    before = (jnp.dot(oh.astype(BF16), earlier, preferred_element_type=F32)
              + jnp.tile(carry, (1, tr // LANES)))
    rank0 = jnp.sum(before * oh0, axis=0, keepdims=True)
    rank1 = jnp.sum(before * oh1, axis=0, keepdims=True)
    carry_ref[...] = carry + jnp.sum(oh, axis=1, keepdims=True)

    st_ref[:, pl.ds(pl.multiple_of(i * tr, tr), tr)] = _rows8(
        (e0.astype(F32), e1.astype(F32), rank0, rank1), tr)
    gate_ref[...] = jnp.concatenate([_rows8((g0, g1), tr), jnp.zeros((LANES - 8, tr), F32)], axis=0).T

    @pl.when(i == nt - 1)
    def _():
        cnt = carry_ref[...]
        nblk = jnp.floor((cnt + (ROW_BLOCK - 1)) * (1.0 / ROW_BLOCK))
        er = lax.broadcasted_iota(jnp.int32, cnt.shape, 0)
        ec = lax.broadcasted_iota(jnp.int32, cnt.shape, 1)
        nblk_row = jnp.sum(jnp.where(er == ec, nblk, 0.0), axis=0, keepdims=True)
        pstart = jnp.sum(jnp.where(ec < er, nblk_row, 0.0), axis=1, keepdims=True)
        pend = pstart + nblk[:, 0:1]
        bidx = lax.broadcasted_iota(jnp.int32, (N_EXPERTS, blk_ref.shape[1]), 1).astype(F32)
        block_e = jnp.minimum(jnp.sum(jnp.where(pend <= bidx, 1.0, 0.0), axis=0, keepdims=True),
                              N_EXPERTS - 1.0)
        total = jnp.sum(nblk_row, axis=1, keepdims=True)
        blk_ref[...] = _rows8((block_e, jnp.broadcast_to(total, block_e.shape)), blk_ref.shape[1])
        rowf = row.astype(F32)

        def dests(j, c):
            sl = pl.ds(pl.multiple_of(j * tr, tr), tr)
            st = st_ref[:, sl]
            base = lambda e: ROW_BLOCK * jnp.sum(jnp.where(rowf == e, pstart, 0.0), axis=0, keepdims=True)
            plan_ref[:, sl] = _rows8((base(st[0:1]) + st[2:3], base(st[1:2]) + st[3:4]), tr)
            return c

        lax.fori_loop(0, nt, dests, 0)


def _route(logits, tr, n_blocks):
    t = logits.shape[0]
    nt = t // tr
    nbp = -(-n_blocks // LANES) * LANES
    return pl.pallas_call(
        functools.partial(_route_kernel, tr=tr, nt=nt),
        grid=(nt,),
        in_specs=[pl.BlockSpec((tr, LANES), lambda i: (i, 0))],
        out_specs=[pl.BlockSpec((tr, LANES), lambda i: (i, 0)),
                   pl.BlockSpec((8, t), lambda i: (0, 0)),
                   pl.BlockSpec((8, nbp), lambda i: (0, 0))],
        out_shape=[jax.ShapeDtypeStruct((t, LANES), F32),
                   jax.ShapeDtypeStruct((8, t), F32),
                   jax.ShapeDtypeStruct((8, nbp), F32)],
        scratch_shapes=[pltpu.VMEM((8, t), F32), pltpu.VMEM((N_EXPERTS, LANES), F32)],
        compiler_params=_cparams("arbitrary"),
        name="route",
    )(logits)


def _row_tile(ref, r):
    return ref.at[pl.ds(pl.multiple_of(r * SUBLANES, SUBLANES), SUBLANES)]


def _dispatch_kernel(d0_ref, d1_ref, h_ref, xs_in_ref, xs_ref, sem, *, ts):
    del xs_in_ref

    def issue(r, c):
        for d_ref in (d0_ref, d1_ref):
            pltpu.make_async_copy(_row_tile(h_ref, r), _row_tile(xs_ref, d_ref[0, 0, r]), sem).start()
        return c

    lax.fori_loop(0, ts, issue, 0, unroll=ISSUE_UNROLL)
    for k in range(2):
        pltpu.make_async_copy(h_ref, xs_ref.at[pl.ds(0, ts * SUBLANES)], sem).wait()


def _dest_spec(ts):
    return pl.BlockSpec((1, 1, ts), lambda i: (i, 0, 0), memory_space=pltpu.SMEM)


def _dispatch(hp, d0, d1, n_rows, ts):
    xs0 = jnp.zeros((n_rows * SUBLANES, LANES), F32)
    return pl.pallas_call(
        functools.partial(_dispatch_kernel, ts=ts),
        grid=(hp.shape[0] // (ts * SUBLANES),),
        in_specs=[_dest_spec(ts), _dest_spec(ts),
                  pl.BlockSpec((ts * SUBLANES, LANES), lambda i: (i, 0)),
                  pl.BlockSpec(memory_space=pl.ANY)],
        out_specs=pl.BlockSpec(memory_space=pl.ANY),
        out_shape=jax.ShapeDtypeStruct(xs0.shape, F32),
        scratch_shapes=[pltpu.SemaphoreType.DMA(())],
        input_output_aliases={3: 0},
        compiler_params=_cparams("arbitrary"),
        name="dispatch",
    )(d0, d1, hp, xs0)


def _expert_kernel(be_ref, nu_ref, xs_ref, wg_ref, wu_ref, wd_ref, ys_ref, wg_s, wu_s, wd_s):
    b = pl.program_id(0)

    @pl.when(b < nu_ref[0])
    def _():
        prev = be_ref[jnp.maximum(b - 1, 0)]

        @pl.when((b == 0) | (be_ref[b] != prev))
        def _():
            wg_s[...] = wg_ref[0, 0].astype(BF16)
            wu_s[...] = wu_ref[0, 0].astype(BF16)
            wd_s[...] = wd_ref[0, 0].astype(BF16)

        x = _load_row_tiles(xs_ref, ROW_BLOCK).astype(BF16)
        mm = lambda a, w: jnp.dot(a, w, preferred_element_type=F32)
        gt = mm(x, wg_s[...])
        up = mm(x, wu_s[...])
        act = (gt * (1.0 / (1.0 + jnp.exp(-gt))) * up).astype(BF16)
        half = wd_s.shape[1] // 2
        for n0 in (0, half):
            _store_row_tiles(ys_ref, mm(act, wd_s[:, n0:n0 + half]), n0 // LANES)

    @pl.when(b >= nu_ref[0])
    def _():
        ys_ref[...] = jnp.zeros_like(ys_ref)


def _experts(xs, block_e, nb_used, w_gate, w_up, w_down, layer):
    n_blocks = xs.shape[0] // (ROW_BLOCK * SUBLANES)
    _, _, d, de = w_gate.shape
    blk = lambda b, be, nu: (jnp.minimum(b, nu[0] - 1), 0)
    wsel = lambda b, be, nu: (layer, be[jnp.minimum(b, nu[0] - 1)], 0, 0)
    return pl.pallas_call(
        _expert_kernel,
        grid_spec=pltpu.PrefetchScalarGridSpec(
            num_scalar_prefetch=2,
            grid=(n_blocks,),
            in_specs=[pl.BlockSpec((ROW_BLOCK * SUBLANES, LANES), blk),
                      pl.BlockSpec((1, 1, d, de), wsel),
                      pl.BlockSpec((1, 1, d, de), wsel),
                      pl.BlockSpec((1, 1, de, d), wsel)],
            out_specs=pl.BlockSpec((ROW_BLOCK * SUBLANES, LANES), lambda b, be, nu: (b, 0)),
            scratch_shapes=[pltpu.VMEM((d, de), BF16), pltpu.VMEM((d, de), BF16),
                            pltpu.VMEM((de, d), BF16)]),
        out_shape=jax.ShapeDtypeStruct(xs.shape, F32),
        compiler_params=_cparams("arbitrary"),
        name="experts",
    )(block_e, nb_used, xs, w_gate, w_up, w_down)


def _combine_kernel(d0_ref, d1_ref, x_ref, gate_ref, ys_ref, fin_ref, out_ref, ybuf, sem, *, ts, final):
    def issue(r, c):
        for k, d_ref in enumerate((d0_ref, d1_ref)):
            pltpu.make_async_copy(_row_tile(ys_ref, d_ref[0, 0, r]), _row_tile(ybuf, k * ts + r), sem).start()
        return c

    lax.fori_loop(0, ts, issue, 0, unroll=ISSUE_UNROLL)
    pltpu.make_async_copy(ys_ref.at[pl.ds(0, 2 * ts * SUBLANES)], ybuf, sem).wait()
    gate = gate_ref[...]
    out = (x_ref[...] + gate[:, 0:1] * _load_row_tiles(ybuf, ts)
           + gate[:, 1:2] * _load_row_tiles(ybuf, ts, ts * SUBLANES))
    if final:
        out = _rms(out, fin_ref[...])
    out_ref[...] = out


def _combine(x2d, gate, d0, d1, ys, fin, ts, final):
    t, d = x2d.shape
    return pl.pallas_call(
        functools.partial(_combine_kernel, ts=ts, final=final),
        grid=(t // ts,),
        in_specs=[_dest_spec(ts), _dest_spec(ts),
                  pl.BlockSpec((ts, d), lambda i: (i, 0)),
                  pl.BlockSpec((ts, LANES), lambda i: (i, 0)),
                  pl.BlockSpec(memory_space=pl.ANY),
                  pl.BlockSpec((1, d), lambda i: (0, 0))],
        out_specs=pl.BlockSpec((ts, d), lambda i: (i, 0)),
        out_shape=jax.ShapeDtypeStruct((t, d), F32),
        scratch_shapes=[pltpu.VMEM((2 * ts * SUBLANES, LANES), F32), pltpu.SemaphoreType.DMA(())],
        compiler_params=_cparams("arbitrary"),
        name="combine",
    )(d0, d1, x2d, gate, ys, fin)


def _moe(x2d, hp, logits, w_gate, w_up, w_down, layer, fin, final, ts):
    t = x2d.shape[0]
    n_blocks = (2 * t) // ROW_BLOCK + N_EXPERTS
    gate, plan, blk = _route(logits, min(512, t), n_blocks)
    d0 = plan[0].astype(jnp.int32).reshape(t // ts, 1, ts)
    d1 = plan[1].astype(jnp.int32).reshape(t // ts, 1, ts)
    block_e = blk[0, :n_blocks].astype(jnp.int32)
    nb_used = blk[1, 0:1].astype(jnp.int32)
    xs = _dispatch(hp, d0, d1, n_blocks * ROW_BLOCK, ts)
    ys = _experts(xs, block_e, nb_used, w_gate, w_up, w_down, layer)
    return _combine(x2d, gate, d0, d1, ys, fin, ts, final)


def _rope_pairs(t, cs):
    p = t * cs
    return p + pltpu.roll(p, QK_ROPE, axis=1)


def _qkv_kernel(x_ref, kn_ref, an_ref, wdkv_ref, kvn_ref, wuk_ref, wuvt_ref, wqd_ref, qn_ref, wqut_ref,
                cs_ref, cst_ref, k_ref, vt_ref, qt_ref):
    x = x_ref[...]
    xn = x * lax.rsqrt(jnp.mean(x * x, axis=-1, keepdims=True) + EPS)
    cs = cs_ref[...]
    mm = lambda a, w: jnp.dot(a.astype(BF16), w, preferred_element_type=F32)
    lane = lax.broadcasted_iota(jnp.int32, cs.shape, 1)

    ckv = mm(xn * kn_ref[...], wdkv_ref[...])
    c_kv = _rms(ckv[:, :KV_RANK], kvn_ref[...])
    k_rope = jnp.where(lane < QK_ROPE, _rope_pairs(ckv[:, KV_RANK:], cs), 0.0).astype(BF16)
    k_nope = mm(c_kv, wuk_ref[...]).astype(BF16)
    for h in range(N_HEADS):
        k_ref[0, h] = jnp.concatenate([k_nope[:, h * QK_NOPE:(h + 1) * QK_NOPE], k_rope], axis=-1)

    mmt = lambda wt, a: lax.dot_general(wt, a.astype(BF16), (((1,), (1,)), ((), ())),
                                        preferred_element_type=F32)
    ts = x.shape[0]
    vt = mmt(wuvt_ref[...], c_kv).astype(BF16)
    one_row = jnp.where(lax.broadcasted_iota(jnp.int32, (VT_PAD - V_DIM, ts), 0) == 0, 1.0, 0.0).astype(BF16)
    cq = _rms(mm(xn * an_ref[...], wqd_ref[...]), qn_ref[...])
    qt = mmt(wqut_ref[...], cq) * (ATTN_SCALE * LOG2E)
    cst = cst_ref[...]
    nope_w = N_HEADS * QK_NOPE
    zpad = jnp.zeros((QK_PAD - QK_NOPE - QK_ROPE, ts), F32)
    for h in range(N_HEADS):
        vt_ref[0, h] = jnp.concatenate([vt[h * V_DIM:(h + 1) * V_DIM], one_row], axis=0)
        rp = qt[nope_w + h * LANES:nope_w + (h + 1) * LANES] * cst
        rope = rp[:QK_ROPE] + rp[QK_ROPE:]
        qt_ref[0, h] = jnp.concatenate([qt[h * QK_NOPE:(h + 1) * QK_NOPE], rope, zpad], axis=0).astype(BF16)


def _qkv(x2d, batch, seq, kn, an, wdkv, kvn, wuk, wuvt, wqd, qn, wqut, cs, cst, ts):
    t, d = x2d.shape
    tps = seq // ts
    full = lambda a: pl.BlockSpec(a.shape, lambda i: (0,) * a.ndim)
    tspec = lambda rows: pl.BlockSpec((1, N_HEADS, rows, ts), lambda i: (i // tps, 0, 0, i % tps))
    return pl.pallas_call(
        _qkv_kernel,
        grid=(t // ts,),
        in_specs=[pl.BlockSpec((ts, d), lambda i: (i, 0)),
                  full(kn), full(an), full(wdkv), full(kvn), full(wuk), full(wuvt), full(wqd), full(qn),
                  full(wqut), pl.BlockSpec((ts, LANES), lambda i: (i % tps, 0)),
                  pl.BlockSpec((LANES, ts), lambda i: (0, i % tps))],
        out_specs=[pl.BlockSpec((1, N_HEADS, ts, QK_PAD), lambda i: (i // tps, 0, i % tps, 0)),
                   tspec(VT_PAD), tspec(QK_PAD)],
        out_shape=[jax.ShapeDtypeStruct((batch, N_HEADS, seq, QK_PAD), BF16),
                   jax.ShapeDtypeStruct((batch, N_HEADS, VT_PAD, seq), BF16),
                   jax.ShapeDtypeStruct((batch, N_HEADS, QK_PAD, seq), BF16)],
        compiler_params=_cparams("arbitrary"),
        name="qkv_proj",
    )(x2d, kn, an, wdkv, kvn, wuk, wuvt, wqd, qn, wqut, cs, cst)


def _attn_kernel(qt_ref, k_ref, vt_ref, o_ref, acc_ref, *, tq):
    i = pl.program_id(2)
    hps = qt_ref.shape[1]
    acc_ref[...] = jnp.zeros_like(acc_ref)

    def tile(j, ms, masked):
        start = pl.multiple_of(j * tq, tq)
        scores = [jnp.dot(k_ref[0, h, pl.ds(start, tq), :], qt_ref[0, h], preferred_element_type=F32)
                  for h in range(hps)]
        out = []
        for h in range(hps):
            vt = vt_ref[0, h, :, pl.ds(start, tq)]
            s = scores[h]
            if masked:
                kc = lax.broadcasted_iota(jnp.int32, s.shape, 0) // CHUNK
                qc = lax.broadcasted_iota(jnp.int32, s.shape, 1) // CHUNK
                s = jnp.where(kc <= qc, s, NEG_BIG)
            m_new = jnp.maximum(ms[h], jnp.max(s, axis=0, keepdims=True))
            p = jnp.exp2((s - m_new).astype(BF16))
            acc_ref[h] = jnp.exp2(ms[h] - m_new) * acc_ref[h] + jnp.dot(vt, p, preferred_element_type=F32)
            out.append(m_new)
        return tuple(out)

    init = tuple(jnp.full((1, tq), NEG_BIG, F32) for _ in range(hps))
    ms = lax.fori_loop(0, i, lambda j, c: tile(j, c, False), init)
    tile(i, ms, True)
    for h in range(hps):
        acc = acc_ref[h]
        o_ref[0, :, h * V_DIM:(h + 1) * V_DIM] = (acc[:V_DIM] / acc[V_DIM:V_DIM + 1]).T.astype(BF16)


def _attention(qt, kc, vt, tq, hps):
    batch, _, _, seq = qt.shape
    return pl.pallas_call(
        functools.partial(_attn_kernel, tq=tq),
        grid=(batch, N_HEADS // hps, seq // tq),
        in_specs=[pl.BlockSpec((1, hps, QK_PAD, tq), lambda b, h, i: (b, h, 0, i)),
                  pl.BlockSpec((1, hps, seq, QK_PAD), lambda b, h, i: (b, h, 0, 0), pipeline_mode=pl.Buffered(1)),
                  pl.BlockSpec((1, hps, VT_PAD, seq), lambda b, h, i: (b, h, 0, 0), pipeline_mode=pl.Buffered(1))],
        out_specs=pl.BlockSpec((1, tq, hps * V_DIM), lambda b, h, i: (b, i, h)),
        out_shape=jax.ShapeDtypeStruct((batch, seq, N_HEADS * V_DIM), BF16),
        scratch_shapes=[pltpu.VMEM((hps, VT_PAD, tq), F32)],
        compiler_params=_cparams("arbitrary", "arbitrary", "arbitrary"),
        name="attention",
    )(qt, kc, vt)


def _oproj_kernel(o_ref, x_ref, wo_ref, fn_ref, wr_ref, br_ref, x3_ref, hp_ref, lg_ref):
    x3 = x_ref[...] + jnp.dot(o_ref[...], wo_ref[...], preferred_element_type=F32)
    x3_ref[...] = x3
    _ffn_prologue(x3, fn_ref[...], wr_ref[...], br_ref[...], hp_ref, lg_ref)


def _oproj(o2d, x2d, wo, fn, wr, br, ts):
    t, d = x2d.shape
    full = lambda a: pl.BlockSpec(a.shape, lambda i: (0,) * a.ndim)
    return pl.pallas_call(
        _oproj_kernel,
        grid=(t // ts,),
        in_specs=[pl.BlockSpec((ts, o2d.shape[1]), lambda i: (i, 0)),
                  pl.BlockSpec((ts, d), lambda i: (i, 0)),
                  full(wo), full(fn), full(wr), full(br)],
        out_specs=[pl.BlockSpec((ts, d), lambda i: (i, 0)),
                   pl.BlockSpec((ts * SUBLANES, LANES), lambda i: (i, 0)),
                   pl.BlockSpec((ts, LANES), lambda i: (i, 0))],
        out_shape=[jax.ShapeDtypeStruct((t, d), F32),
                   jax.ShapeDtypeStruct((t * SUBLANES, LANES), F32),
                   jax.ShapeDtypeStruct((t, LANES), F32)],
        compiler_params=_cparams("arbitrary"),
        name="out_proj",
    )(o2d, x2d, wo, fn, wr, br)


def _router_params(rg_w, rg_b, re_w, re_b):
    d = rg_w.shape[0]
    used = N_GROUPS + N_EXPERTS
    wr = jnp.concatenate([re_w, rg_w, jnp.zeros((d, LANES - used), F32)], axis=1)
    br = jnp.concatenate([re_b, rg_b, jnp.zeros((LANES - used,), F32)])[None, :]
    return wr, br


def _with_rotate_half(w):
    half = w.shape[-1] // 2
    return jnp.concatenate([w, -w[..., half:], w[..., :half]], axis=-1)


def _rope_table(seq):
    half = QK_ROPE // 2
    inv = ROPE_THETA ** (-jnp.arange(half, dtype=F32) / half)
    ang = jnp.arange(seq, dtype=F32)[:, None] * inv[None, :]
    cos, sin = jnp.cos(ang), jnp.sin(ang)
    return jnp.concatenate([cos, cos, sin, sin], axis=1)


def kernel(x, pool_norm, pool_w, pool_b, pool_scale, kv_in_norm, w_dkv, kv_norm, w_uk, w_uv, attn_norm, wq_down, q_norm, wq_up, wo, ffn_norm, router_group_w, router_group_b, router_expert_w, router_expert_b, w_gate, w_up, w_down, final_norm):
    batch, seq, d = x.shape
    t = batch * seq
    depth = ffn_norm.shape[0]
    n_a = pool_norm.shape[0]
    ts = min(256, seq)
    tq = min(512, seq)
    row = lambda a: a.reshape(1, -1)

    cs = _rope_table(seq)
    routers = [_router_params(router_group_w[l], router_group_b[l], router_expert_w[l], router_expert_b[l])
               for l in range(depth)]
    wdkv = jnp.concatenate([w_dkv[:, :KV_RANK], _with_rotate_half(w_dkv[:, KV_RANK:])], axis=1).astype(BF16)
    wuk, wuvt = w_uk.astype(BF16), w_uv.T.astype(BF16)
    cst = cs.T

    x2d = x.reshape(t, d)
    kc = vv = None
    for l in range(depth):
        wr, br = routers[l]
        if l < n_a:
            x2d, hp, logits = _pool_layer(x2d, seq, row(pool_norm[l]), pool_w[l].astype(BF16), pool_b[l][:, None, :],
                                          row(pool_scale[l]), row(ffn_norm[l]), wr, br, ts)
        else:
            j = l - n_a
            wqu = wq_up[j].reshape(-1, N_HEADS, QK_NOPE + QK_ROPE)
            wqu = jnp.concatenate([wqu[:, :, :QK_NOPE].reshape(-1, N_HEADS * QK_NOPE),
                                   _with_rotate_half(wqu[:, :, QK_NOPE:]).reshape(-1, N_HEADS * LANES)],
                                  axis=1).T.astype(BF16)
            k_new, v_new, qt = _qkv(x2d, batch, seq, row(kv_in_norm), row(attn_norm[j]), wdkv, row(kv_norm),
                                    wuk, wuvt, wq_down[j].astype(BF16), row(q_norm[j]), wqu, cs, cst, ts)
            if kc is None:
                kc, vv = k_new, v_new
            o = _attention(qt, kc, vv, tq, 4)
            x2d, hp, logits = _oproj(o.reshape(t, -1), x2d, wo[j].astype(BF16), row(ffn_norm[l]), wr, br, ts)
        x2d = _moe(x2d, hp, logits, w_gate, w_up, w_down, l, row(final_norm), l == depth - 1, ts)
    return x2d.reshape(batch, seq, d)
```

```python
import functools
import math

import jax
import jax.numpy as jnp
from jax import lax
from jax.experimental import pallas as pl
from jax.experimental.pallas import tpu as pltpu

EPS = 1e-6
CHUNK = 64
POOL_WINDOWS = (2, 4, 8, 16)
N_HEADS = 8
QK_NOPE = 128
QK_ROPE = 64
V_DIM = 128
KV_RANK = 256
ROPE_THETA = 10000.0
ATTN_SCALE = 1.0 / math.sqrt(QK_NOPE + QK_ROPE)
N_GROUPS = 4
EXPERTS_PER_GROUP = 8
N_EXPERTS = N_GROUPS * EXPERTS_PER_GROUP
ROW_BLOCK = 256

LANES = 128
SUBLANES = 8
HALO = 16
QK_PAD = 256
VT_PAD = V_DIM + 16
LOG2E = math.log2(math.e)
VMEM_LIMIT = 56 * 1024 * 1024
NEG_BIG = -1e30
ISSUE_UNROLL = 8

F32 = jnp.float32
BF16 = jnp.bfloat16


def _cparams(*sem):
    return pltpu.CompilerParams(dimension_semantics=sem, vmem_limit_bytes=VMEM_LIMIT)


def _rms(x, g):
    return x * lax.rsqrt(jnp.mean(x * x, axis=-1, keepdims=True) + EPS) * g


def _store_row_tiles(ref, val, chunk0=0):
    n = val.shape[0]
    for c in range(val.shape[1] // LANES):
        ref[pl.ds(chunk0 + c, n, stride=SUBLANES), :] = val[:, c * LANES:(c + 1) * LANES]


def _load_row_tiles(ref, n, base=0):
    return jnp.concatenate([ref[pl.ds(base + c, n, stride=SUBLANES), :] for c in range(SUBLANES)], axis=1)


def _split_bf16(a):
    hi = a.astype(BF16)
    return hi, (a - hi.astype(F32)).astype(BF16)


def _ffn_prologue(x, fn, wr_hi, wr_lo, br, hp_ref, lg_ref):
    h = _rms(x, fn)
    _store_row_tiles(hp_ref, h)
    h_hi, h_lo = _split_bf16(h)
    mm = lambda a, w: jnp.dot(a, w, preferred_element_type=F32)
    lg_ref[...] = mm(h_hi, wr_hi) + mm(h_lo, wr_hi) + mm(h_hi, wr_lo) + br


def _pool_kernel(x_ref, pn_ref, pw_ref, pb_ref, ps_ref, fn_ref, wr_ref, br_ref,
                 x1_ref, hp_ref, lg_ref, buf_ref, *, ts, tiles_per_seq):
    i = pl.program_id(0)
    seq_tile = i % tiles_per_seq
    x = x_ref[...]
    h = _rms(x, pn_ref[...])

    @pl.when(seq_tile == 0)
    def _():
        buf_ref[0:HALO, :] = jnp.zeros((HALO, x.shape[1]), F32)

    buf_ref[HALO:, :] = h
    pos = lax.broadcasted_iota(jnp.int32, (ts, 1), 0) + seq_tile * ts
    gd = x.shape[1] // len(POOL_WINDOWS)
    for g, win in enumerate(POOL_WINDOWS):
        cols = slice(g * gd, (g + 1) * gd)
        s = buf_ref[:, cols]
        k = 1
        while k < win:
            s = s + pltpu.roll(s, k, axis=0)
            k *= 2
        cnt = jnp.minimum(pos + 1, win).astype(F32)
        pooled = s[HALO:, :] / cnt - h[:, cols]
        y = jnp.dot(pooled.astype(BF16), pw_ref[g], preferred_element_type=F32) + pb_ref[g]
        x1_ref[:, cols] = x[:, cols] + y * ps_ref[:, cols]
    buf_ref[0:HALO, :] = h[ts - HALO:, :]
    _ffn_prologue(x1_ref[...], fn_ref[...], wr_ref[0], wr_ref[1], br_ref[...], hp_ref, lg_ref)


def _pool_layer(x2d, seq, pn, pw, pb, ps, fn, wr, br, ts):
    t, d = x2d.shape
    full = lambda *shape: pl.BlockSpec(shape, lambda i: (0,) * len(shape))
    return pl.pallas_call(
        functools.partial(_pool_kernel, ts=ts, tiles_per_seq=seq // ts),
        grid=(t // ts,),
        in_specs=[pl.BlockSpec((ts, d), lambda i: (i, 0)),
                  full(1, d), full(*pw.shape), full(*pb.shape), full(1, d), full(1, d),
                  full(2, d, LANES), full(1, LANES)],
        out_specs=[pl.BlockSpec((ts, d), lambda i: (i, 0)),
                   pl.BlockSpec((ts * SUBLANES, LANES), lambda i: (i, 0)),
                   pl.BlockSpec((ts, LANES), lambda i: (i, 0))],
        out_shape=[jax.ShapeDtypeStruct((t, d), F32),
                   jax.ShapeDtypeStruct((t * SUBLANES, LANES), F32),
                   jax.ShapeDtypeStruct((t, LANES), F32)],
        scratch_shapes=[pltpu.VMEM((HALO + ts, d), F32)],
        compiler_params=_cparams("arbitrary"),
        name="pool_layer",
    )(x2d, pn, pw, pb, ps, fn, wr, br)


def _rows8(vals, width):
    row8 = lax.broadcasted_iota(jnp.int32, (8, width), 0)
    out = jnp.zeros((8, width), F32)
    for c, v in enumerate(vals):
        out = jnp.where(row8 == c, v, out)
    return out


def _route_kernel(lg_ref, gate_ref, plan_ref, blk_ref, st_ref, carry_ref, *, tr, nt):
    i = pl.program_id(0)

    @pl.when(i == 0)
    def _():
        carry_ref[...] = jnp.zeros_like(carry_ref)

    lgt = lg_ref[...].T
    neg = jnp.float32(-jnp.inf)
    row8 = lax.broadcasted_iota(jnp.int32, (8, tr), 0)
    row = lax.broadcasted_iota(jnp.int32, (N_EXPERTS, tr), 0)
    first = lambda hit, idx, n: jnp.min(jnp.where(hit, idx, n), axis=0, keepdims=True)

    gl = jnp.where(row8 < N_GROUPS, lgt[N_EXPERTS:N_EXPERTS + 8], neg)
    gmax = jnp.max(gl, axis=0, keepdims=True)
    gidx = first(gl == gmax, row8, 8)
    gprob = 1.0 / jnp.sum(jnp.exp(gl - gmax), axis=0, keepdims=True)

    el = jnp.where(row // EXPERTS_PER_GROUP == gidx, lgt[0:N_EXPERTS], neg)
    m1 = jnp.max(el, axis=0, keepdims=True)
    e0 = first(el == m1, row, N_EXPERTS)
    el2 = jnp.where(row == e0, neg, el)
    m2 = jnp.max(el2, axis=0, keepdims=True)
    e1 = first(el2 == m2, row, N_EXPERTS)
    r = jnp.exp(m2 - m1)
    g0 = gprob / (1.0 + r)
    g1 = gprob * r / (1.0 + r)

    oh0 = jnp.where(row == e0, 1.0, 0.0)
    oh1 = jnp.where(row == e1, 1.0, 0.0)
    oh = oh0 + oh1
    rr = lax.broadcasted_iota(jnp.int32, (tr, tr), 0)
    cc = lax.broadcasted_iota(jnp.int32, (tr, tr), 1)
    earlier = jnp.where(rr < cc, 1.0, 0.0).astype(BF16)
    carry = carry_ref[...]
    before = (jnp.dot(oh.astype(BF16), earlier, preferred_element_type=F32)
              + jnp.tile(carry, (1, tr // LANES)))
    rank0 = jnp.sum(before * oh0, axis=0, keepdims=True)
    rank1 = jnp.sum(before * oh1, axis=0, keepdims=True)
    carry_ref[...] = carry + jnp.sum(oh, axis=1, keepdims=True)

    st_ref[:, pl.ds(pl.multiple_of(i * tr, tr), tr)] = _rows8(
        (e0.astype(F32), e1.astype(F32), rank0, rank1), tr)
    gate_ref[...] = jnp.concatenate([_rows8((g0, g1), tr), jnp.zeros((LANES - 8, tr), F32)], axis=0).T

    @pl.when(i == nt - 1)
    def _():
        cnt = carry_ref[...]
        nblk = jnp.floor((cnt + (ROW_BLOCK - 1)) * (1.0 / ROW_BLOCK))
        er = lax.broadcasted_iota(jnp.int32, cnt.shape, 0)
        ec = lax.broadcasted_iota(jnp.int32, cnt.shape, 1)
        nblk_row = jnp.sum(jnp.where(er == ec, nblk, 0.0), axis=0, keepdims=True)
        pstart = jnp.sum(jnp.where(ec < er, nblk_row, 0.0), axis=1, keepdims=True)
        pend = pstart + nblk[:, 0:1]
        bidx = lax.broadcasted_iota(jnp.int32, (N_EXPERTS, blk_ref.shape[1]), 1).astype(F32)
        block_e = jnp.minimum(jnp.sum(jnp.where(pend <= bidx, 1.0, 0.0), axis=0, keepdims=True),
                              N_EXPERTS - 1.0)
        total = jnp.sum(nblk_row, axis=1, keepdims=True)
        blk_ref[...] = _rows8((block_e, jnp.broadcast_to(total, block_e.shape)), blk_ref.shape[1])
        rowf = row.astype(F32)

        def dests(j, c):
            sl = pl.ds(pl.multiple_of(j * tr, tr), tr)
            st = st_ref[:, sl]
            base = lambda e: ROW_BLOCK * jnp.sum(jnp.where(rowf == e, pstart, 0.0), axis=0, keepdims=True)
            plan_ref[:, sl] = _rows8((base(st[0:1]) + st[2:3], base(st[1:2]) + st[3:4]), tr)
            return c

        lax.fori_loop(0, nt, dests, 0)


def _route(logits, tr, n_blocks):
    t = logits.shape[0]
    nt = t // tr
    nbp = -(-n_blocks // LANES) * LANES
    return pl.pallas_call(
        functools.partial(_route_kernel, tr=tr, nt=nt),
        grid=(nt,),
        in_specs=[pl.BlockSpec((tr, LANES), lambda i: (i, 0))],
        out_specs=[pl.BlockSpec((tr, LANES), lambda i: (i, 0)),
                   pl.BlockSpec((8, t), lambda i: (0, 0)),
                   pl.BlockSpec((8, nbp), lambda i: (0, 0))],
        out_shape=[jax.ShapeDtypeStruct((t, LANES), F32),
                   jax.ShapeDtypeStruct((8, t), F32),
                   jax.ShapeDtypeStruct((8, nbp), F32)],
        scratch_shapes=[pltpu.VMEM((8, t), F32), pltpu.VMEM((N_EXPERTS, LANES), F32)],
        compiler_params=_cparams("arbitrary"),
        name="route",
    )(logits)


def _row_tile(ref, r):
    return ref.at[pl.ds(pl.multiple_of(r * SUBLANES, SUBLANES), SUBLANES)]


def _dispatch_kernel(d0_ref, d1_ref, h_ref, xs_in_ref, xs_ref, sem, *, ts):
    del xs_in_ref

    def issue(r, c):
        for k, d_ref in enumerate((d0_ref, d1_ref)):
            pltpu.make_async_copy(_row_tile(h_ref, r), _row_tile(xs_ref, d_ref[0, 0, r]), sem).start(priority=k)
        return c

    lax.fori_loop(0, ts, issue, 0, unroll=ISSUE_UNROLL)
    for k in range(2):
        pltpu.make_async_copy(h_ref, xs_ref.at[pl.ds(0, ts * SUBLANES)], sem).wait()


def _dest_spec(ts):
    return pl.BlockSpec((1, 1, ts), lambda i: (i, 0, 0), memory_space=pltpu.SMEM)


def _dispatch(hp, d0, d1, n_rows, ts):
    xs0 = jnp.zeros((n_rows * SUBLANES, LANES), F32)
    return pl.pallas_call(
        functools.partial(_dispatch_kernel, ts=ts),
        grid=(hp.shape[0] // (ts * SUBLANES),),
        in_specs=[_dest_spec(ts), _dest_spec(ts),
                  pl.BlockSpec((ts * SUBLANES, LANES), lambda i: (i, 0)),
                  pl.BlockSpec(memory_space=pl.ANY)],
        out_specs=pl.BlockSpec(memory_space=pl.ANY),
        out_shape=jax.ShapeDtypeStruct(xs0.shape, F32),
        scratch_shapes=[pltpu.SemaphoreType.DMA(())],
        input_output_aliases={3: 0},
        compiler_params=_cparams("arbitrary"),
        name="dispatch",
    )(d0, d1, hp, xs0)


def _expert_kernel(be_ref, nu_ref, xs_ref, wg_ref, wu_ref, wd_ref, ys_ref, wg_s, wu_s, wd_s):
    b = pl.program_id(0)

    @pl.when(b < nu_ref[0])
    def _():
        prev = be_ref[jnp.maximum(b - 1, 0)]

        @pl.when((b == 0) | (be_ref[b] != prev))
        def _():
            wg_s[...] = wg_ref[0, 0].astype(BF16)
            wu_s[...] = wu_ref[0, 0].astype(BF16)
            wd_s[...] = wd_ref[0, 0].astype(BF16)

        x = _load_row_tiles(xs_ref, ROW_BLOCK).astype(BF16)
        mm = lambda a, w: jnp.dot(a, w, preferred_element_type=F32)
        gt = mm(x, wg_s[...])
        up = mm(x, wu_s[...])
        act = (gt * (1.0 / (1.0 + jnp.exp(-gt))) * up).astype(BF16)
        half = wd_s.shape[1] // 2
        for n0 in (0, half):
            _store_row_tiles(ys_ref, mm(act, wd_s[:, n0:n0 + half]), n0 // LANES)

    @pl.when(b >= nu_ref[0])
    def _():
        ys_ref[...] = jnp.zeros_like(ys_ref)


def _experts(xs, block_e, nb_used, w_gate, w_up, w_down, layer):
    n_blocks = xs.shape[0] // (ROW_BLOCK * SUBLANES)
    _, _, d, de = w_gate.shape
    blk = lambda b, be, nu: (jnp.minimum(b, nu[0] - 1), 0)
    wsel = lambda b, be, nu: (layer, be[jnp.minimum(b, nu[0] - 1)], 0, 0)
    return pl.pallas_call(
        _expert_kernel,
        grid_spec=pltpu.PrefetchScalarGridSpec(
            num_scalar_prefetch=2,
            grid=(n_blocks,),
            in_specs=[pl.BlockSpec((ROW_BLOCK * SUBLANES, LANES), blk),
                      pl.BlockSpec((1, 1, d, de), wsel),
                      pl.BlockSpec((1, 1, d, de), wsel),
                      pl.BlockSpec((1, 1, de, d), wsel)],
            out_specs=pl.BlockSpec((ROW_BLOCK * SUBLANES, LANES), lambda b, be, nu: (b, 0)),
            scratch_shapes=[pltpu.VMEM((d, de), BF16), pltpu.VMEM((d, de), BF16),
                            pltpu.VMEM((de, d), BF16)]),
        out_shape=jax.ShapeDtypeStruct(xs.shape, F32),
        compiler_params=_cparams("arbitrary"),
        name="experts",
    )(block_e, nb_used, xs, w_gate, w_up, w_down)


def _combine_kernel(d0_ref, d1_ref, x_ref, gate_ref, ys_ref, fin_ref, out_ref, ybuf, sem, *, ts, final):
    def issue(r, c):
        for k, d_ref in enumerate((d0_ref, d1_ref)):
            pltpu.make_async_copy(_row_tile(ys_ref, d_ref[0, 0, r]), _row_tile(ybuf, k * ts + r),
                                  sem).start(priority=k)
        return c

    lax.fori_loop(0, ts, issue, 0, unroll=ISSUE_UNROLL)
    pltpu.make_async_copy(ys_ref.at[pl.ds(0, 2 * ts * SUBLANES)], ybuf, sem).wait()
    gate = gate_ref[...]
    out = (x_ref[...] + gate[:, 0:1] * _load_row_tiles(ybuf, ts)
           + gate[:, 1:2] * _load_row_tiles(ybuf, ts, ts * SUBLANES))
    if final:
        out = _rms(out, fin_ref[...])
    out_ref[...] = out


def _combine(x2d, gate, d0, d1, ys, fin, ts, final):
    t, d = x2d.shape
    return pl.pallas_call(
        functools.partial(_combine_kernel, ts=ts, final=final),
        grid=(t // ts,),
        in_specs=[_dest_spec(ts), _dest_spec(ts),
                  pl.BlockSpec((ts, d), lambda i: (i, 0)),
                  pl.BlockSpec((ts, LANES), lambda i: (i, 0)),
                  pl.BlockSpec(memory_space=pl.ANY),
                  pl.BlockSpec((1, d), lambda i: (0, 0))],
        out_specs=pl.BlockSpec((ts, d), lambda i: (i, 0)),
        out_shape=jax.ShapeDtypeStruct((t, d), F32),
        scratch_shapes=[pltpu.VMEM((2 * ts * SUBLANES, LANES), F32), pltpu.SemaphoreType.DMA(())],
        compiler_params=_cparams("arbitrary"),
        name="combine",
    )(d0, d1, x2d, gate, ys, fin)


def _moe(x2d, hp, logits, w_gate, w_up, w_down, layer, fin, final, ts):
    t = x2d.shape[0]
    n_blocks = (2 * t) // ROW_BLOCK + N_EXPERTS
    gate, plan, blk = _route(logits, min(512, t), n_blocks)
    d0 = plan[0].astype(jnp.int32).reshape(t // ts, 1, ts)
    d1 = plan[1].astype(jnp.int32).reshape(t // ts, 1, ts)
    block_e = blk[0, :n_blocks].astype(jnp.int32)
    nb_used = blk[1, 0:1].astype(jnp.int32)
    xs = _dispatch(hp, d0, d1, n_blocks * ROW_BLOCK, ts)
    ys = _experts(xs, block_e, nb_used, w_gate, w_up, w_down, layer)
    return _combine(x2d, gate, d0, d1, ys, fin, ts, final)


def _rope_pairs(t, cs):
    p = t * cs
    return p + pltpu.roll(p, QK_ROPE, axis=1)


def _qkv_kernel(x_ref, kn_ref, an_ref, wdkv_ref, kvn_ref, wuk_ref, wuvt_ref, wqd_ref, qn_ref, wqut_ref,
                cs_ref, cst_ref, k_ref, vt_ref, qt_ref):
    x = x_ref[...]
    xn = x * lax.rsqrt(jnp.mean(x * x, axis=-1, keepdims=True) + EPS)
    cs = cs_ref[...]
    mm = lambda a, w: jnp.dot(a.astype(BF16), w, preferred_element_type=F32)
    lane = lax.broadcasted_iota(jnp.int32, cs.shape, 1)

    ckv = mm(xn * kn_ref[...], wdkv_ref[...])
    c_kv = _rms(ckv[:, :KV_RANK], kvn_ref[...])
    k_rope = jnp.where(lane < QK_ROPE, _rope_pairs(ckv[:, KV_RANK:], cs), 0.0).astype(BF16)
    k_nope = mm(c_kv, wuk_ref[...]).astype(BF16)
    for h in range(N_HEADS):
        k_ref[0, h] = jnp.concatenate([k_nope[:, h * QK_NOPE:(h + 1) * QK_NOPE], k_rope], axis=-1)

    mmt = lambda wt, a: lax.dot_general(wt, a.astype(BF16), (((1,), (1,)), ((), ())),
                                        preferred_element_type=F32)
    ts = x.shape[0]
    vt = mmt(wuvt_ref[...], c_kv).astype(BF16)
    one_row = jnp.where(lax.broadcasted_iota(jnp.int32, (VT_PAD - V_DIM, ts), 0) == 0, 1.0, 0.0).astype(BF16)
    cq = _rms(mm(xn * an_ref[...], wqd_ref[...]), qn_ref[...])
    qt = mmt(wqut_ref[...], cq) * (ATTN_SCALE * LOG2E)
    cst = cst_ref[...]
    nope_w = N_HEADS * QK_NOPE
    zpad = jnp.zeros((QK_PAD - QK_NOPE - QK_ROPE, ts), F32)
    for h in range(N_HEADS):
        vt_ref[0, h] = jnp.concatenate([vt[h * V_DIM:(h + 1) * V_DIM], one_row], axis=0)
        rp = qt[nope_w + h * LANES:nope_w + (h + 1) * LANES] * cst
        rope = rp[:QK_ROPE] + rp[QK_ROPE:]
        qt_ref[0, h] = jnp.concatenate([qt[h * QK_NOPE:(h + 1) * QK_NOPE], rope, zpad], axis=0).astype(BF16)


def _qkv(x2d, batch, seq, kn, an, wdkv, kvn, wuk, wuvt, wqd, qn, wqut, cs, cst, ts):
    t, d = x2d.shape
    tps = seq // ts
    full = lambda a: pl.BlockSpec(a.shape, lambda i: (0,) * a.ndim)
    tspec = lambda rows: pl.BlockSpec((1, N_HEADS, rows, ts), lambda i: (i // tps, 0, 0, i % tps))
    return pl.pallas_call(
        _qkv_kernel,
        grid=(t // ts,),
        in_specs=[pl.BlockSpec((ts, d), lambda i: (i, 0)),
                  full(kn), full(an), full(wdkv), full(kvn), full(wuk), full(wuvt), full(wqd), full(qn),
                  full(wqut), pl.BlockSpec((ts, LANES), lambda i: (i % tps, 0)),
                  pl.BlockSpec((LANES, ts), lambda i: (0, i % tps))],
        out_specs=[pl.BlockSpec((1, N_HEADS, ts, QK_PAD), lambda i: (i // tps, 0, i % tps, 0)),
                   tspec(VT_PAD), tspec(QK_PAD)],
        out_shape=[jax.ShapeDtypeStruct((batch, N_HEADS, seq, QK_PAD), BF16),
                   jax.ShapeDtypeStruct((batch, N_HEADS, VT_PAD, seq), BF16),
                   jax.ShapeDtypeStruct((batch, N_HEADS, QK_PAD, seq), BF16)],
        compiler_params=_cparams("arbitrary"),
        name="qkv_proj",
    )(x2d, kn, an, wdkv, kvn, wuk, wuvt, wqd, qn, wqut, cs, cst)


def _attn_kernel(qt_ref, k_ref, vt_ref, o_ref, acc_ref, *, tq):
    i = pl.program_id(2)
    hps = qt_ref.shape[1]
    acc_ref[...] = jnp.zeros_like(acc_ref)

    def tile(j, ms, masked):
        start = pl.multiple_of(j * tq, tq)
        scores = [jnp.dot(k_ref[0, h, pl.ds(start, tq), :], qt_ref[0, h], preferred_element_type=F32)
                  for h in range(hps)]
        out = []
        for h in range(hps):
            vt = vt_ref[0, h, :, pl.ds(start, tq)]
            s = scores[h]
            if masked:
                kc = lax.broadcasted_iota(jnp.int32, s.shape, 0) // CHUNK
                qc = lax.broadcasted_iota(jnp.int32, s.shape, 1) // CHUNK
                s = jnp.where(kc <= qc, s, NEG_BIG)
            m_new = jnp.maximum(ms[h], jnp.max(s, axis=0, keepdims=True))
            p = jnp.exp2((s - m_new).astype(BF16))
            acc_ref[h] = jnp.exp2(ms[h] - m_new) * acc_ref[h] + jnp.dot(vt, p, preferred_element_type=F32)
            out.append(m_new)
        return tuple(out)

    init = tuple(jnp.full((1, tq), NEG_BIG, F32) for _ in range(hps))
    ms = lax.fori_loop(0, i, lambda j, c: tile(j, c, False), init)
    tile(i, ms, True)
    for h in range(hps):
        acc = acc_ref[h]
        o_ref[0, :, h * V_DIM:(h + 1) * V_DIM] = (acc[:V_DIM] / acc[V_DIM:V_DIM + 1]).T.astype(BF16)


def _attention(qt, kc, vt, tq, hps):
    batch, _, _, seq = qt.shape
    return pl.pallas_call(
        functools.partial(_attn_kernel, tq=tq),
        grid=(batch, N_HEADS // hps, seq // tq),
        in_specs=[pl.BlockSpec((1, hps, QK_PAD, tq), lambda b, h, i: (b, h, 0, i)),
                  pl.BlockSpec((1, hps, seq, QK_PAD), lambda b, h, i: (b, h, 0, 0), pipeline_mode=pl.Buffered(1)),
                  pl.BlockSpec((1, hps, VT_PAD, seq), lambda b, h, i: (b, h, 0, 0), pipeline_mode=pl.Buffered(1))],
        out_specs=pl.BlockSpec((1, tq, hps * V_DIM), lambda b, h, i: (b, i, h)),
        out_shape=jax.ShapeDtypeStruct((batch, seq, N_HEADS * V_DIM), BF16),
        scratch_shapes=[pltpu.VMEM((hps, VT_PAD, tq), F32)],
        compiler_params=_cparams("arbitrary", "arbitrary", "arbitrary"),
        name="attention",
    )(qt, kc, vt)


def _oproj_kernel(o_ref, x_ref, wo_ref, fn_ref, wr_ref, br_ref, x3_ref, hp_ref, lg_ref):
    x3 = x_ref[...] + jnp.dot(o_ref[...], wo_ref[...], preferred_element_type=F32)
    x3_ref[...] = x3
    _ffn_prologue(x3, fn_ref[...], wr_ref[0], wr_ref[1], br_ref[...], hp_ref, lg_ref)


def _oproj(o2d, x2d, wo, fn, wr, br, ts):
    t, d = x2d.shape
    full = lambda a: pl.BlockSpec(a.shape, lambda i: (0,) * a.ndim)
    return pl.pallas_call(
        _oproj_kernel,
        grid=(t // ts,),
        in_specs=[pl.BlockSpec((ts, o2d.shape[1]), lambda i: (i, 0)),
                  pl.BlockSpec((ts, d), lambda i: (i, 0)),
                  full(wo), full(fn), full(wr), full(br)],
        out_specs=[pl.BlockSpec((ts, d), lambda i: (i, 0)),
                   pl.BlockSpec((ts * SUBLANES, LANES), lambda i: (i, 0)),
                   pl.BlockSpec((ts, LANES), lambda i: (i, 0))],
        out_shape=[jax.ShapeDtypeStruct((t, d), F32),
                   jax.ShapeDtypeStruct((t * SUBLANES, LANES), F32),
                   jax.ShapeDtypeStruct((t, LANES), F32)],
        compiler_params=_cparams("arbitrary"),
        name="out_proj",
    )(o2d, x2d, wo, fn, wr, br)


def _router_params(rg_w, rg_b, re_w, re_b):
    d = rg_w.shape[0]
    used = N_GROUPS + N_EXPERTS
    wr = jnp.concatenate([re_w, rg_w, jnp.zeros((d, LANES - used), F32)], axis=1)
    br = jnp.concatenate([re_b, rg_b, jnp.zeros((LANES - used,), F32)])[None, :]
    wr_hi = wr.astype(BF16)
    wr_lo = (wr - wr_hi.astype(F32)).astype(BF16)
    return jnp.stack([wr_hi, wr_lo]), br


def _with_rotate_half(w):
    half = w.shape[-1] // 2
    return jnp.concatenate([w, -w[..., half:], w[..., :half]], axis=-1)


def _rope_table(seq):
    half = QK_ROPE // 2
    inv = ROPE_THETA ** (-jnp.arange(half, dtype=F32) / half)
    ang = jnp.arange(seq, dtype=F32)[:, None] * inv[None, :]
    cos, sin = jnp.cos(ang), jnp.sin(ang)
    return jnp.concatenate([cos, cos, sin, sin], axis=1)


def kernel(x, pool_norm, pool_w, pool_b, pool_scale, kv_in_norm, w_dkv, kv_norm, w_uk, w_uv, attn_norm, wq_down, q_norm, wq_up, wo, ffn_norm, router_group_w, router_group_b, router_expert_w, router_expert_b, w_gate, w_up, w_down, final_norm):
    batch, seq, d = x.shape
    t = batch * seq
    depth = ffn_norm.shape[0]
    n_a = pool_norm.shape[0]
    ts = min(256, seq)
    tq = min(512, seq)
    row = lambda a: a.reshape(1, -1)

    cs = _rope_table(seq)
    routers = [_router_params(router_group_w[l], router_group_b[l], router_expert_w[l], router_expert_b[l])
               for l in range(depth)]
    wdkv = jnp.concatenate([w_dkv[:, :KV_RANK], _with_rotate_half(w_dkv[:, KV_RANK:])], axis=1).astype(BF16)
    wuk, wuvt = w_uk.astype(BF16), w_uv.T.astype(BF16)
    cst = cs.T

    x2d = x.reshape(t, d)
    kc = vv = None
    for l in range(depth):
        wr, br = routers[l]
        if l < n_a:
            x2d, hp, logits = _pool_layer(x2d, seq, row(pool_norm[l]), pool_w[l].astype(BF16), pool_b[l][:, None, :],
                                          row(pool_scale[l]), row(ffn_norm[l]), wr, br, ts)
        else:
            j = l - n_a
            wqu = wq_up[j].reshape(-1, N_HEADS, QK_NOPE + QK_ROPE)
            wqu = jnp.concatenate([wqu[:, :, :QK_NOPE].reshape(-1, N_HEADS * QK_NOPE),
                                   _with_rotate_half(wqu[:, :, QK_NOPE:]).reshape(-1, N_HEADS * LANES)],
                                  axis=1).T.astype(BF16)
            k_new, v_new, qt = _qkv(x2d, batch, seq, row(kv_in_norm), row(attn_norm[j]), wdkv, row(kv_norm),
                                    wuk, wuvt, wq_down[j].astype(BF16), row(q_norm[j]), wqu, cs, cst, ts)
            if kc is None:
                kc, vv = k_new, v_new
            o = _attention(qt, kc, vv, tq, 4)
            x2d, hp, logits = _oproj(o.reshape(t, -1), x2d, wo[j].astype(BF16), row(ffn_norm[l]), wr, br, ts)
        x2d = _moe(x2d, hp, logits, w_gate, w_up, w_down, l, row(final_norm), l == depth - 1, ts)
    return x2d.reshape(batch, seq, d)
```

```python
import functools
import math

import jax
import jax.numpy as jnp
from jax import lax
from jax.experimental import pallas as pl
from jax.experimental.pallas import tpu as pltpu

EPS = 1e-6
CHUNK = 64
POOL_WINDOWS = (2, 4, 8, 16)
N_HEADS = 8
QK_NOPE = 128
QK_ROPE = 64
V_DIM = 128
KV_RANK = 256
ROPE_THETA = 10000.0
ATTN_SCALE = 1.0 / math.sqrt(QK_NOPE + QK_ROPE)
N_GROUPS = 4
EXPERTS_PER_GROUP = 8
N_EXPERTS = N_GROUPS * EXPERTS_PER_GROUP
ROW_BLOCK = 256

LANES = 128
SUBLANES = 8
PACK_ROWS = SUBLANES // 2
HALO = 16
QK_PAD = 256
VT_PAD = V_DIM + 16
LOG2E = math.log2(math.e)
VMEM_LIMIT = 56 * 1024 * 1024
NEG_BIG = -1e30
ISSUE_UNROLL = 8

F32 = jnp.float32
BF16 = jnp.bfloat16


def _cparams(*sem):
    return pltpu.CompilerParams(dimension_semantics=sem, vmem_limit_bytes=VMEM_LIMIT)


def _rms(x, g):
    return x * lax.rsqrt(jnp.mean(x * x, axis=-1, keepdims=True) + EPS) * g


def _store_row_tiles(ref, val, chunk0=0):
    n = val.shape[0]
    for c in range(val.shape[1] // LANES):
        ref[pl.ds(chunk0 + c, n, stride=SUBLANES), :] = val[:, c * LANES:(c + 1) * LANES]


def _load_row_tiles(ref, n, base=0):
    return jnp.concatenate([ref[pl.ds(base + c, n, stride=SUBLANES), :] for c in range(SUBLANES)], axis=1)


def _store_packed_rows(ref, val):
    n = val.shape[0]
    bits = pltpu.bitcast(val.astype(F32), jnp.uint32)
    for c in range(PACK_ROWS):
        lo = bits[:, c * LANES:(c + 1) * LANES] >> 16
        hi = bits[:, (c + PACK_ROWS) * LANES:(c + PACK_ROWS + 1) * LANES]
        ref[pl.ds(c, n, stride=PACK_ROWS), :] = lo | hi


def _load_packed_rows(ref, n):
    words = [ref[pl.ds(c, n, stride=PACK_ROWS), :] for c in range(PACK_ROWS)]
    lo = [pltpu.bitcast(w << 16, F32).astype(BF16) for w in words]
    hi = [pltpu.bitcast(w & jnp.uint32(0xFFFF0000), F32).astype(BF16) for w in words]
    return jnp.concatenate(lo + hi, axis=1)


def _packed_rows(ref, r, n=1):
    return ref.at[pl.ds(pl.multiple_of(r * PACK_ROWS, PACK_ROWS), n * PACK_ROWS)]


def _split_bf16(a):
    hi = a.astype(BF16)
    return hi, (a - hi.astype(F32)).astype(BF16)


def _ffn_prologue(x, fn, wr_hi, wr_lo, br, hp_ref, lg_ref):
    h = _rms(x, fn)
    h_hi, h_lo = _split_bf16(h)
    _store_packed_rows(hp_ref, h_hi)
    mm = lambda a, w: jnp.dot(a, w, preferred_element_type=F32)
    lg_ref[...] = mm(h_hi, wr_hi) + mm(h_lo, wr_hi) + mm(h_hi, wr_lo) + br


def _pool_kernel(x_ref, pn_ref, pw_ref, pb_ref, ps_ref, fn_ref, wr_ref, br_ref,
                 x1_ref, hp_ref, lg_ref, buf_ref, *, ts, tiles_per_seq):
    i = pl.program_id(0)
    seq_tile = i % tiles_per_seq
    x = x_ref[...]
    h = _rms(x, pn_ref[...])

    @pl.when(seq_tile == 0)
    def _():
        buf_ref[0:HALO, :] = jnp.zeros((HALO, x.shape[1]), F32)

    buf_ref[HALO:, :] = h
    pos = lax.broadcasted_iota(jnp.int32, (ts, 1), 0) + seq_tile * ts
    gd = x.shape[1] // len(POOL_WINDOWS)
    for g, win in enumerate(POOL_WINDOWS):
        cols = slice(g * gd, (g + 1) * gd)
        s = buf_ref[:, cols]
        k = 1
        while k < win:
            s = s + pltpu.roll(s, k, axis=0)
            k *= 2
        cnt = jnp.minimum(pos + 1, win).astype(F32)
        pooled = s[HALO:, :] / cnt - h[:, cols]
        y = jnp.dot(pooled.astype(BF16), pw_ref[g], preferred_element_type=F32) + pb_ref[g]
        x1_ref[:, cols] = x[:, cols] + y * ps_ref[:, cols]
    buf_ref[0:HALO, :] = h[ts - HALO:, :]
    _ffn_prologue(x1_ref[...], fn_ref[...], wr_ref[0], wr_ref[1], br_ref[...], hp_ref, lg_ref)


def _pool_layer(x2d, seq, pn, pw, pb, ps, fn, wr, br, ts):
    t, d = x2d.shape
    full = lambda *shape: pl.BlockSpec(shape, lambda i: (0,) * len(shape))
    return pl.pallas_call(
        functools.partial(_pool_kernel, ts=ts, tiles_per_seq=seq // ts),
        grid=(t // ts,),
        in_specs=[pl.BlockSpec((ts, d), lambda i: (i, 0)),
                  full(1, d), full(*pw.shape), full(*pb.shape), full(1, d), full(1, d),
                  full(2, d, LANES), full(1, LANES)],
        out_specs=[pl.BlockSpec((ts, d), lambda i: (i, 0)),
                   pl.BlockSpec((ts * PACK_ROWS, LANES), lambda i: (i, 0)),
                   pl.BlockSpec((ts, LANES), lambda i: (i, 0))],
        out_shape=[jax.ShapeDtypeStruct((t, d), F32),
                   jax.ShapeDtypeStruct((t * PACK_ROWS, LANES), jnp.uint32),
                   jax.ShapeDtypeStruct((t, LANES), F32)],
        scratch_shapes=[pltpu.VMEM((HALO + ts, d), F32)],
        compiler_params=_cparams("arbitrary"),
        name="pool_layer",
    )(x2d, pn, pw, pb, ps, fn, wr, br)


def _rows8(vals, width):
    row8 = lax.broadcasted_iota(jnp.int32, (8, width), 0)
    out = jnp.zeros((8, width), F32)
    for c, v in enumerate(vals):
        out = jnp.where(row8 == c, v, out)
    return out


def _route_kernel(lg_ref, gate_ref, plan_ref, blk_ref, st_ref, carry_ref, *, tr, nt):
    i = pl.program_id(0)

    @pl.when(i == 0)
    def _():
        carry_ref[...] = jnp.zeros_like(carry_ref)

    lgt = lg_ref[...].T
    neg = jnp.float32(-jnp.inf)
    row8 = lax.broadcasted_iota(jnp.int32, (8, tr), 0)
    row = lax.broadcasted_iota(jnp.int32, (N_EXPERTS, tr), 0)
    first = lambda hit, idx, n: jnp.min(jnp.where(hit, idx, n), axis=0, keepdims=True)

    gl = jnp.where(row8 < N_GROUPS, lgt[N_EXPERTS:N_EXPERTS + 8], neg)
    gmax = jnp.max(gl, axis=0, keepdims=True)
    gidx = first(gl == gmax, row8, 8)
    gprob = 1.0 / jnp.sum(jnp.exp(gl - gmax), axis=0, keepdims=True)

    el = jnp.where(row // EXPERTS_PER_GROUP == gidx, lgt[0:N_EXPERTS], neg)
    m1 = jnp.max(el, axis=0, keepdims=True)
    e0 = first(el == m1, row, N_EXPERTS)
    el2 = jnp.where(row == e0, neg, el)
    m2 = jnp.max(el2, axis=0, keepdims=True)
    e1 = first(el2 == m2, row, N_EXPERTS)
    r = jnp.exp(m2 - m1)
    g0 = gprob / (1.0 + r)
    g1 = gprob * r / (1.0 + r)

    oh0 = jnp.where(row == e0, 1.0, 0.0)
    oh1 = jnp.where(row == e1, 1.0, 0.0)
    oh = oh0 + oh1
    rr = lax.broadcasted_iota(jnp.int32, (tr, tr), 0)
    cc = lax.broadcasted_iota(jnp.int32, (tr, tr), 1)
    earlier = jnp.where(rr < cc, 1.0, 0.0).astype(BF16)
    carry = carry_ref[...]
    before = (jnp.dot(oh.astype(BF16), earlier, preferred_element_type=F32)
              + jnp.tile(carry, (1, tr // LANES)))
    rank0 = jnp.sum(before * oh0, axis=0, keepdims=True)
    rank1 = jnp.sum(before * oh1, axis=0, keepdims=True)
    carry_ref[...] = carry + jnp.sum(oh, axis=1, keepdims=True)

    st_ref[:, pl.ds(pl.multiple_of(i * tr, tr), tr)] = _rows8(
        (e0.astype(F32), e1.astype(F32), rank0, rank1), tr)
    gate_ref[...] = jnp.concatenate([_rows8((g0, g1), tr), jnp.zeros((LANES - 8, tr), F32)], axis=0).T

    @pl.when(i == nt - 1)
    def _():
        cnt = carry_ref[...]
        nblk = jnp.floor((cnt + (ROW_BLOCK - 1)) * (1.0 / ROW_BLOCK))
        er = lax.broadcasted_iota(jnp.int32, cnt.shape, 0)
        ec = lax.broadcasted_iota(jnp.int32, cnt.shape, 1)
        nblk_row = jnp.sum(jnp.where(er == ec, nblk, 0.0), axis=0, keepdims=True)
        pstart = jnp.sum(jnp.where(ec < er, nblk_row, 0.0), axis=1, keepdims=True)
        pend = pstart + nblk[:, 0:1]
        bidx = lax.broadcasted_iota(jnp.int32, (N_EXPERTS, blk_ref.shape[1]), 1).astype(F32)
        block_e = jnp.minimum(jnp.sum(jnp.where(pend <= bidx, 1.0, 0.0), axis=0, keepdims=True),
                              N_EXPERTS - 1.0)
        total = jnp.sum(nblk_row, axis=1, keepdims=True)
        lane_row = lambda col: jnp.concatenate(
            [jnp.sum(jnp.where(er == ec, col, 0.0), axis=0, keepdims=True),
             jnp.zeros((1, blk_ref.shape[1] - LANES), F32)], axis=1)
        pad_at = lane_row(pstart * ROW_BLOCK + cnt)
        pad_len = lane_row(nblk * ROW_BLOCK - cnt)
        blk_ref[...] = _rows8((block_e, jnp.broadcast_to(total, block_e.shape), pad_at, pad_len),
                              blk_ref.shape[1])
        rowf = row.astype(F32)

        def dests(j, c):
            sl = pl.ds(pl.multiple_of(j * tr, tr), tr)
            st = st_ref[:, sl]
            base = lambda e: ROW_BLOCK * jnp.sum(jnp.where(rowf == e, pstart, 0.0), axis=0, keepdims=True)
            plan_ref[:, sl] = _rows8((base(st[0:1]) + st[2:3], base(st[1:2]) + st[3:4]), tr)
            return c

        lax.fori_loop(0, nt, dests, 0)


def _route(logits, tr, n_blocks):
    t = logits.shape[0]
    nt = t // tr
    nbp = -(-n_blocks // LANES) * LANES
    return pl.pallas_call(
        functools.partial(_route_kernel, tr=tr, nt=nt),
        grid=(nt,),
        in_specs=[pl.BlockSpec((tr, LANES), lambda i: (i, 0))],
        out_specs=[pl.BlockSpec((tr, LANES), lambda i: (i, 0)),
                   pl.BlockSpec((8, t), lambda i: (0, 0)),
                   pl.BlockSpec((8, nbp), lambda i: (0, 0))],
        out_shape=[jax.ShapeDtypeStruct((t, LANES), F32),
                   jax.ShapeDtypeStruct((8, t), F32),
                   jax.ShapeDtypeStruct((8, nbp), F32)],
        scratch_shapes=[pltpu.VMEM((8, t), F32), pltpu.VMEM((N_EXPERTS, LANES), F32)],
        compiler_params=_cparams("arbitrary"),
        name="route",
    )(logits)


def _row_tile(ref, r):
    return ref.at[pl.ds(pl.multiple_of(r * SUBLANES, SUBLANES), SUBLANES)]


def _dispatch_kernel(pad_at_ref, pad_len_ref, nu_ref, d0_ref, d1_ref, h_ref, xs_ref, zero_ref, sem, zsem,
                     *, ts, n_blocks):
    first_step = pl.program_id(0) == 0

    def zero_fill(wait):
        def go(cp):
            cp.wait() if wait else cp.start()

        def segment(e, c):
            at, n = pad_at_ref[e], pad_len_ref[e]
            for bit in reversed(range(ROW_BLOCK.bit_length() - 1)):
                size = 1 << bit

                @pl.when((n & size) != 0)
                def _():
                    go(pltpu.make_async_copy(_packed_rows(zero_ref, 0, size),
                                             _packed_rows(xs_ref, at + (n & ~(2 * size - 1)), size), zsem))
            return c

        def tail_block(b, c):
            go(pltpu.make_async_copy(zero_ref, _packed_rows(xs_ref, b * ROW_BLOCK, ROW_BLOCK), zsem))
            return c

        lax.fori_loop(0, N_EXPERTS, segment, 0)
        lax.fori_loop(nu_ref[0], n_blocks, tail_block, 0)

    @pl.when(first_step)
    def _():
        zero_ref[...] = jnp.zeros_like(zero_ref)
        zero_fill(wait=False)

    def issue(r, c):
        for k, d_ref in enumerate((d0_ref, d1_ref)):
            pltpu.make_async_copy(_packed_rows(h_ref, r), _packed_rows(xs_ref, d_ref[0, 0, r]),
                                  sem).start(priority=k)
        return c

    lax.fori_loop(0, ts, issue, 0, unroll=ISSUE_UNROLL)
    for k in range(2):
        pltpu.make_async_copy(h_ref, _packed_rows(xs_ref, 0, ts), sem).wait()

    @pl.when(first_step)
    def _():
        zero_fill(wait=True)


def _dest_spec(ts):
    return pl.BlockSpec((1, 1, ts), lambda i, *_: (i, 0, 0), memory_space=pltpu.SMEM)


def _dispatch(hp, d0, d1, pad_at, pad_len, nb_used, n_blocks, ts):
    return pl.pallas_call(
        functools.partial(_dispatch_kernel, ts=ts, n_blocks=n_blocks),
        grid_spec=pltpu.PrefetchScalarGridSpec(
            num_scalar_prefetch=3,
            grid=(hp.shape[0] // (ts * PACK_ROWS),),
            in_specs=[_dest_spec(ts), _dest_spec(ts),
                      pl.BlockSpec((ts * PACK_ROWS, LANES), lambda i, *_: (i, 0))],
            out_specs=pl.BlockSpec(memory_space=pl.ANY),
            scratch_shapes=[pltpu.VMEM((ROW_BLOCK * PACK_ROWS, LANES), jnp.uint32),
                            pltpu.SemaphoreType.DMA(()), pltpu.SemaphoreType.DMA(())]),
        out_shape=jax.ShapeDtypeStruct((n_blocks * ROW_BLOCK * PACK_ROWS, LANES), jnp.uint32),
        compiler_params=_cparams("arbitrary"),
        name="dispatch",
    )(pad_at, pad_len, nb_used, d0, d1, hp)


def _expert_kernel(be_ref, nu_ref, xs_ref, wg_ref, wu_ref, wd_ref, ys_ref, wg_s, wu_s, wd_s):
    b = pl.program_id(0)

    @pl.when(b < nu_ref[0])
    def _():
        prev = be_ref[jnp.maximum(b - 1, 0)]

        @pl.when((b == 0) | (be_ref[b] != prev))
        def _():
            wg_s[...] = wg_ref[0, 0].astype(BF16)
            wu_s[...] = wu_ref[0, 0].astype(BF16)
            wd_s[...] = wd_ref[0, 0].astype(BF16)

        x = _load_packed_rows(xs_ref, ROW_BLOCK)
        mm = lambda a, w: jnp.dot(a, w, preferred_element_type=F32)
        gt = mm(x, wg_s[...])
        up = mm(x, wu_s[...])
        act = (gt * (1.0 / (1.0 + jnp.exp(-gt))) * up).astype(BF16)
        half = wd_s.shape[1] // 2
        for n0 in (0, half):
            _store_row_tiles(ys_ref, mm(act, wd_s[:, n0:n0 + half]), n0 // LANES)

    @pl.when(b >= nu_ref[0])
    def _():
        ys_ref[...] = jnp.zeros_like(ys_ref)


def _experts(xs, block_e, nb_used, w_gate, w_up, w_down, layer):
    n_blocks = xs.shape[0] // (ROW_BLOCK * PACK_ROWS)
    _, _, d, de = w_gate.shape
    blk = lambda b, be, nu: (jnp.minimum(b, nu[0] - 1), 0)
    wsel = lambda b, be, nu: (layer, be[jnp.minimum(b, nu[0] - 1)], 0, 0)
    return pl.pallas_call(
        _expert_kernel,
        grid_spec=pltpu.PrefetchScalarGridSpec(
            num_scalar_prefetch=2,
            grid=(n_blocks,),
            in_specs=[pl.BlockSpec((ROW_BLOCK * PACK_ROWS, LANES), blk),
                      pl.BlockSpec((1, 1, d, de), wsel),
                      pl.BlockSpec((1, 1, d, de), wsel),
                      pl.BlockSpec((1, 1, de, d), wsel)],
            out_specs=pl.BlockSpec((ROW_BLOCK * SUBLANES, LANES), lambda b, be, nu: (b, 0)),
            scratch_shapes=[pltpu.VMEM((d, de), BF16), pltpu.VMEM((d, de), BF16),
                            pltpu.VMEM((de, d), BF16)]),
        out_shape=jax.ShapeDtypeStruct((n_blocks * ROW_BLOCK * SUBLANES, LANES), F32),
        compiler_params=_cparams("arbitrary"),
        name="experts",
    )(block_e, nb_used, xs, w_gate, w_up, w_down)


def _combine_kernel(d0_ref, d1_ref, x_ref, gate_ref, ys_ref, fin_ref, out_ref, ybuf, sem, *, ts, final):
    def issue(r, c):
        for k, d_ref in enumerate((d0_ref, d1_ref)):
            pltpu.make_async_copy(_row_tile(ys_ref, d_ref[0, 0, r]), _row_tile(ybuf, k * ts + r),
                                  sem).start(priority=k)
        return c

    lax.fori_loop(0, ts, issue, 0, unroll=ISSUE_UNROLL)
    pltpu.make_async_copy(ys_ref.at[pl.ds(0, 2 * ts * SUBLANES)], ybuf, sem).wait()
    gate = gate_ref[...]
    out = (x_ref[...] + gate[:, 0:1] * _load_row_tiles(ybuf, ts)
           + gate[:, 1:2] * _load_row_tiles(ybuf, ts, ts * SUBLANES))
    if final:
        out = _rms(out, fin_ref[...])
    out_ref[...] = out


def _combine(x2d, gate, d0, d1, ys, fin, ts, final):
    t, d = x2d.shape
    return pl.pallas_call(
        functools.partial(_combine_kernel, ts=ts, final=final),
        grid=(t // ts,),
        in_specs=[_dest_spec(ts), _dest_spec(ts),
                  pl.BlockSpec((ts, d), lambda i: (i, 0)),
                  pl.BlockSpec((ts, LANES), lambda i: (i, 0)),
                  pl.BlockSpec(memory_space=pl.ANY),
                  pl.BlockSpec((1, d), lambda i: (0, 0))],
        out_specs=pl.BlockSpec((ts, d), lambda i: (i, 0)),
        out_shape=jax.ShapeDtypeStruct((t, d), F32),
        scratch_shapes=[pltpu.VMEM((2 * ts * SUBLANES, LANES), F32), pltpu.SemaphoreType.DMA(())],
        compiler_params=_cparams("arbitrary"),
        name="combine",
    )(d0, d1, x2d, gate, ys, fin)


def _moe(x2d, hp, logits, w_gate, w_up, w_down, layer, fin, final, ts):
    t = x2d.shape[0]
    n_blocks = (2 * t) // ROW_BLOCK + N_EXPERTS
    gate, plan, blk = _route(logits, min(512, t), n_blocks)
    d0 = plan[0].astype(jnp.int32).reshape(t // ts, 1, ts)
    d1 = plan[1].astype(jnp.int32).reshape(t // ts, 1, ts)
    block_e = blk[0, :n_blocks].astype(jnp.int32)
    nb_used = blk[1, 0:1].astype(jnp.int32)
    pad_at = blk[2, :N_EXPERTS].astype(jnp.int32)
    pad_len = blk[3, :N_EXPERTS].astype(jnp.int32)
    xs = _dispatch(hp, d0, d1, pad_at, pad_len, nb_used, n_blocks, ts)
    ys = _experts(xs, block_e, nb_used, w_gate, w_up, w_down, layer)
    return _combine(x2d, gate, d0, d1, ys, fin, ts, final)


def _rope_pairs(t, cs):
    p = t * cs
    return p + pltpu.roll(p, QK_ROPE, axis=1)


def _qkv_kernel(x_ref, kn_ref, an_ref, wdkv_ref, kvn_ref, wuk_ref, wuvt_ref, wqd_ref, qn_ref, wqut_ref,
                cs_ref, cst_ref, k_ref, vt_ref, qt_ref):
    x = x_ref[...]
    xn = x * lax.rsqrt(jnp.mean(x * x, axis=-1, keepdims=True) + EPS)
    cs = cs_ref[...]
    mm = lambda a, w: jnp.dot(a.astype(BF16), w, preferred_element_type=F32)
    lane = lax.broadcasted_iota(jnp.int32, cs.shape, 1)

    ckv = mm(xn * kn_ref[...], wdkv_ref[...])
    c_kv = _rms(ckv[:, :KV_RANK], kvn_ref[...])
    k_rope = jnp.where(lane < QK_ROPE, _rope_pairs(ckv[:, KV_RANK:], cs), 0.0).astype(BF16)
    k_nope = mm(c_kv, wuk_ref[...]).astype(BF16)
    for h in range(N_HEADS):
        k_ref[0, h] = jnp.concatenate([k_nope[:, h * QK_NOPE:(h + 1) * QK_NOPE], k_rope], axis=-1)

    mmt = lambda wt, a: lax.dot_general(wt, a.astype(BF16), (((1,), (1,)), ((), ())),
                                        preferred_element_type=F32)
    ts = x.shape[0]
    vt = mmt(wuvt_ref[...], c_kv).astype(BF16)
    one_row = jnp.where(lax.broadcasted_iota(jnp.int32, (VT_PAD - V_DIM, ts), 0) == 0, 1.0, 0.0).astype(BF16)
    cq = _rms(mm(xn * an_ref[...], wqd_ref[...]), qn_ref[...])
    qt = mmt(wqut_ref[...], cq) * (ATTN_SCALE * LOG2E)
    cst = cst_ref[...]
    nope_w = N_HEADS * QK_NOPE
    zpad = jnp.zeros((QK_PAD - QK_NOPE - QK_ROPE, ts), F32)
    for h in range(N_HEADS):
        vt_ref[0, h] = jnp.concatenate([vt[h * V_DIM:(h + 1) * V_DIM], one_row], axis=0)
        rp = qt[nope_w + h * LANES:nope_w + (h + 1) * LANES] * cst
        rope = rp[:QK_ROPE] + rp[QK_ROPE:]
        qt_ref[0, h] = jnp.concatenate([qt[h * QK_NOPE:(h + 1) * QK_NOPE], rope, zpad], axis=0).astype(BF16)


def _qkv(x2d, batch, seq, kn, an, wdkv, kvn, wuk, wuvt, wqd, qn, wqut, cs, cst, ts):
    t, d = x2d.shape
    tps = seq // ts
    full = lambda a: pl.BlockSpec(a.shape, lambda i: (0,) * a.ndim)
    tspec = lambda rows: pl.BlockSpec((1, N_HEADS, rows, ts), lambda i: (i // tps, 0, 0, i % tps))
    return pl.pallas_call(
        _qkv_kernel,
        grid=(t // ts,),
        in_specs=[pl.BlockSpec((ts, d), lambda i: (i, 0)),
                  full(kn), full(an), full(wdkv), full(kvn), full(wuk), full(wuvt), full(wqd), full(qn),
                  full(wqut), pl.BlockSpec((ts, LANES), lambda i: (i % tps, 0)),
                  pl.BlockSpec((LANES, ts), lambda i: (0, i % tps))],
        out_specs=[pl.BlockSpec((1, N_HEADS, ts, QK_PAD), lambda i: (i // tps, 0, i % tps, 0)),
                   tspec(VT_PAD), tspec(QK_PAD)],
        out_shape=[jax.ShapeDtypeStruct((batch, N_HEADS, seq, QK_PAD), BF16),
                   jax.ShapeDtypeStruct((batch, N_HEADS, VT_PAD, seq), BF16),
                   jax.ShapeDtypeStruct((batch, N_HEADS, QK_PAD, seq), BF16)],
        compiler_params=_cparams("arbitrary"),
        name="qkv_proj",
    )(x2d, kn, an, wdkv, kvn, wuk, wuvt, wqd, qn, wqut, cs, cst)


def _attn_kernel(qt_ref, k_ref, vt_ref, o_ref, acc_ref, *, tq):
    i = pl.program_id(2)
    hps = qt_ref.shape[1]
    acc_ref[...] = jnp.zeros_like(acc_ref)

    def tile(j, ms, masked):
        start = pl.multiple_of(j * tq, tq)
        scores = [jnp.dot(k_ref[0, h, pl.ds(start, tq), :], qt_ref[0, h], preferred_element_type=F32)
                  for h in range(hps)]
        out = []
        for h in range(hps):
            vt = vt_ref[0, h, :, pl.ds(start, tq)]
            s = scores[h]
            if masked:
                kc = lax.broadcasted_iota(jnp.int32, s.shape, 0) // CHUNK
                qc = lax.broadcasted_iota(jnp.int32, s.shape, 1) // CHUNK
                s = jnp.where(kc <= qc, s, NEG_BIG)
            m_new = jnp.maximum(ms[h], jnp.max(s, axis=0, keepdims=True))
            p = jnp.exp2((s - m_new).astype(BF16))
            acc_ref[h] = jnp.exp2(ms[h] - m_new) * acc_ref[h] + jnp.dot(vt, p, preferred_element_type=F32)
            out.append(m_new)
        return tuple(out)

    init = tuple(jnp.full((1, tq), NEG_BIG, F32) for _ in range(hps))
    ms = lax.fori_loop(0, i, lambda j, c: tile(j, c, False), init)
    tile(i, ms, True)
    for h in range(hps):
        acc = acc_ref[h]
        o_ref[0, :, h * V_DIM:(h + 1) * V_DIM] = (acc[:V_DIM] / acc[V_DIM:V_DIM + 1]).T.astype(BF16)


def _attention(qt, kc, vt, tq, hps):
    batch, _, _, seq = qt.shape
    return pl.pallas_call(
        functools.partial(_attn_kernel, tq=tq),
        grid=(batch, N_HEADS // hps, seq // tq),
        in_specs=[pl.BlockSpec((1, hps, QK_PAD, tq), lambda b, h, i: (b, h, 0, i)),
                  pl.BlockSpec((1, hps, seq, QK_PAD), lambda b, h, i: (b, h, 0, 0), pipeline_mode=pl.Buffered(1)),
                  pl.BlockSpec((1, hps, VT_PAD, seq), lambda b, h, i: (b, h, 0, 0), pipeline_mode=pl.Buffered(1))],
        out_specs=pl.BlockSpec((1, tq, hps * V_DIM), lambda b, h, i: (b, i, h)),
        out_shape=jax.ShapeDtypeStruct((batch, seq, N_HEADS * V_DIM), BF16),
        scratch_shapes=[pltpu.VMEM((hps, VT_PAD, tq), F32)],
        compiler_params=_cparams("arbitrary", "arbitrary", "arbitrary"),
        name="attention",
    )(qt, kc, vt)


def _oproj_kernel(o_ref, x_ref, wo_ref, fn_ref, wr_ref, br_ref, x3_ref, hp_ref, lg_ref):
    x3 = x_ref[...] + jnp.dot(o_ref[...], wo_ref[...], preferred_element_type=F32)
    x3_ref[...] = x3
    _ffn_prologue(x3, fn_ref[...], wr_ref[0], wr_ref[1], br_ref[...], hp_ref, lg_ref)


def _oproj(o2d, x2d, wo, fn, wr, br, ts):
    t, d = x2d.shape
    full = lambda a: pl.BlockSpec(a.shape, lambda i: (0,) * a.ndim)
    return pl.pallas_call(
        _oproj_kernel,
        grid=(t // ts,),
        in_specs=[pl.BlockSpec((ts, o2d.shape[1]), lambda i: (i, 0)),
                  pl.BlockSpec((ts, d), lambda i: (i, 0)),
                  full(wo), full(fn), full(wr), full(br)],
        out_specs=[pl.BlockSpec((ts, d), lambda i: (i, 0)),
                   pl.BlockSpec((ts * PACK_ROWS, LANES), lambda i: (i, 0)),
                   pl.BlockSpec((ts, LANES), lambda i: (i, 0))],
        out_shape=[jax.ShapeDtypeStruct((t, d), F32),
                   jax.ShapeDtypeStruct((t * PACK_ROWS, LANES), jnp.uint32),
                   jax.ShapeDtypeStruct((t, LANES), F32)],
        compiler_params=_cparams("arbitrary"),
        name="out_proj",
    )(o2d, x2d, wo, fn, wr, br)


def _router_params(rg_w, rg_b, re_w, re_b):
    d = rg_w.shape[0]
    used = N_GROUPS + N_EXPERTS
    wr = jnp.concatenate([re_w, rg_w, jnp.zeros((d, LANES - used), F32)], axis=1)
    br = jnp.concatenate([re_b, rg_b, jnp.zeros((LANES - used,), F32)])[None, :]
    wr_hi = wr.astype(BF16)
    wr_lo = (wr - wr_hi.astype(F32)).astype(BF16)
    return jnp.stack([wr_hi, wr_lo]), br


def _with_rotate_half(w):
    half = w.shape[-1] // 2
    return jnp.concatenate([w, -w[..., half:], w[..., :half]], axis=-1)


def _rope_table(seq):
    half = QK_ROPE // 2
    inv = ROPE_THETA ** (-jnp.arange(half, dtype=F32) / half)
    ang = jnp.arange(seq, dtype=F32)[:, None] * inv[None, :]
    cos, sin = jnp.cos(ang), jnp.sin(ang)
    return jnp.concatenate([cos, cos, sin, sin], axis=1)


def kernel(x, pool_norm, pool_w, pool_b, pool_scale, kv_in_norm, w_dkv, kv_norm, w_uk, w_uv, attn_norm, wq_down, q_norm, wq_up, wo, ffn_norm, router_group_w, router_group_b, router_expert_w, router_expert_b, w_gate, w_up, w_down, final_norm):
    batch, seq, d = x.shape
    t = batch * seq
    depth = ffn_norm.shape[0]
    n_a = pool_norm.shape[0]
    ts = min(256, seq)
    tq = min(512, seq)
    row = lambda a: a.reshape(1, -1)

    cs = _rope_table(seq)
    routers = [_router_params(router_group_w[l], router_group_b[l], router_expert_w[l], router_expert_b[l])
               for l in range(depth)]
    wdkv = jnp.concatenate([w_dkv[:, :KV_RANK], _with_rotate_half(w_dkv[:, KV_RANK:])], axis=1).astype(BF16)
    wuk, wuvt = w_uk.astype(BF16), w_uv.T.astype(BF16)
    cst = cs.T

    x2d = x.reshape(t, d)
    kc = vv = None
    for l in range(depth):
        wr, br = routers[l]
        if l < n_a:
            x2d, hp, logits = _pool_layer(x2d, seq, row(pool_norm[l]), pool_w[l].astype(BF16), pool_b[l][:, None, :],
                                          row(pool_scale[l]), row(ffn_norm[l]), wr, br, ts)
        else:
            j = l - n_a
            wqu = wq_up[j].reshape(-1, N_HEADS, QK_NOPE + QK_ROPE)
            wqu = jnp.concatenate([wqu[:, :, :QK_NOPE].reshape(-1, N_HEADS * QK_NOPE),
                                   _with_rotate_half(wqu[:, :, QK_NOPE:]).reshape(-1, N_HEADS * LANES)],
                                  axis=1).T.astype(BF16)
            k_new, v_new, qt = _qkv(x2d, batch, seq, row(kv_in_norm), row(attn_norm[j]), wdkv, row(kv_norm),
                                    wuk, wuvt, wq_down[j].astype(BF16), row(q_norm[j]), wqu, cs, cst, ts)
            if kc is None:
                kc, vv = k_new, v_new
            o = _attention(qt, kc, vv, tq, 4)
            x2d, hp, logits = _oproj(o.reshape(t, -1), x2d, wo[j].astype(BF16), row(ffn_norm[l]), wr, br, ts)
        x2d = _moe(x2d, hp, logits, w_gate, w_up, w_down, l, row(final_norm), l == depth - 1, ts)
    return x2d.reshape(batch, seq, d)
```

```python
import functools
import math

import jax
import jax.numpy as jnp
from jax import lax
from jax.experimental import pallas as pl
from jax.experimental.pallas import tpu as pltpu

EPS = 1e-6
CHUNK = 64
POOL_WINDOWS = (2, 4, 8, 16)
N_HEADS = 8
QK_NOPE = 128
QK_ROPE = 64
V_DIM = 128
KV_RANK = 256
ROPE_THETA = 10000.0
ATTN_SCALE = 1.0 / math.sqrt(QK_NOPE + QK_ROPE)
N_GROUPS = 4
EXPERTS_PER_GROUP = 8
N_EXPERTS = N_GROUPS * EXPERTS_PER_GROUP
ROW_BLOCK = 256

LANES = 128
SUBLANES = 8
PACK_ROWS = SUBLANES // 2
HALO = 16
QK_PAD = 256
VT_PAD = V_DIM + 16
LOG2E = math.log2(math.e)
VMEM_LIMIT = 56 * 1024 * 1024
NEG_BIG = -1e30
QK_AHEAD = 4
ISSUE_UNROLL = 8

F32 = jnp.float32
BF16 = jnp.bfloat16


def _cparams(*sem):
    return pltpu.CompilerParams(dimension_semantics=sem, vmem_limit_bytes=VMEM_LIMIT)


def _rms(x, g):
    return x * lax.rsqrt(jnp.mean(x * x, axis=-1, keepdims=True) + EPS) * g


def _store_row_tiles(ref, val, chunk0=0):
    n = val.shape[0]
    for c in range(val.shape[1] // LANES):
        ref[pl.ds(chunk0 + c, n, stride=SUBLANES), :] = val[:, c * LANES:(c + 1) * LANES]


def _load_row_tiles(ref, n, base=0):
    return jnp.concatenate([ref[pl.ds(base + c, n, stride=SUBLANES), :] for c in range(SUBLANES)], axis=1)


def _store_packed_rows(ref, val):
    n = val.shape[0]
    bits = pltpu.bitcast(val.astype(F32), jnp.uint32)
    for c in range(PACK_ROWS):
        lo = bits[:, c * LANES:(c + 1) * LANES] >> 16
        hi = bits[:, (c + PACK_ROWS) * LANES:(c + PACK_ROWS + 1) * LANES]
        ref[pl.ds(c, n, stride=PACK_ROWS), :] = lo | hi


def _load_packed_rows(ref, n):
    words = [ref[pl.ds(c, n, stride=PACK_ROWS), :] for c in range(PACK_ROWS)]
    lo = [pltpu.bitcast(w << 16, F32).astype(BF16) for w in words]
    hi = [pltpu.bitcast(w & jnp.uint32(0xFFFF0000), F32).astype(BF16) for w in words]
    return jnp.concatenate(lo + hi, axis=1)


def _packed_rows(ref, r, n=1):
    return ref.at[pl.ds(pl.multiple_of(r * PACK_ROWS, PACK_ROWS), n * PACK_ROWS)]


def _split_bf16(a):
    hi = a.astype(BF16)
    return hi, (a - hi.astype(F32)).astype(BF16)


def _ffn_prologue(x, fn, wr_hi, wr_lo, br, hp_ref, lg_ref):
    h = _rms(x, fn)
    h_hi, h_lo = _split_bf16(h)
    _store_packed_rows(hp_ref, h_hi)
    mm = lambda a, w: jnp.dot(a, w, preferred_element_type=F32)
    lg_ref[...] = mm(h_hi, wr_hi) + mm(h_lo, wr_hi) + mm(h_hi, wr_lo) + br


def _pool_kernel(x_ref, pn_ref, pw_ref, pb_ref, ps_ref, fn_ref, wr_ref, br_ref,
                 x1_ref, hp_ref, lg_ref, buf_ref, *, ts, tiles_per_seq):
    i = pl.program_id(0)
    seq_tile = i % tiles_per_seq
    x = x_ref[...]
    h = _rms(x, pn_ref[...])

    @pl.when(seq_tile == 0)
    def _():
        buf_ref[0:HALO, :] = jnp.zeros((HALO, x.shape[1]), F32)

    buf_ref[HALO:, :] = h
    pos = lax.broadcasted_iota(jnp.int32, (ts, 1), 0) + seq_tile * ts
    gd = x.shape[1] // len(POOL_WINDOWS)
    for g, win in enumerate(POOL_WINDOWS):
        cols = slice(g * gd, (g + 1) * gd)
        s = buf_ref[:, cols]
        k = 1
        while k < win:
            s = s + pltpu.roll(s, k, axis=0)
            k *= 2
        cnt = jnp.minimum(pos + 1, win).astype(F32)
        pooled = s[HALO:, :] / cnt - h[:, cols]
        y = jnp.dot(pooled.astype(BF16), pw_ref[g], preferred_element_type=F32) + pb_ref[g]
        x1_ref[:, cols] = x[:, cols] + y * ps_ref[:, cols]
    buf_ref[0:HALO, :] = h[ts - HALO:, :]
    _ffn_prologue(x1_ref[...], fn_ref[...], wr_ref[0], wr_ref[1], br_ref[...], hp_ref, lg_ref)


def _pool_layer(x2d, seq, pn, pw, pb, ps, fn, wr, br, ts):
    t, d = x2d.shape
    full = lambda *shape: pl.BlockSpec(shape, lambda i: (0,) * len(shape))
    return pl.pallas_call(
        functools.partial(_pool_kernel, ts=ts, tiles_per_seq=seq // ts),
        grid=(t // ts,),
        in_specs=[pl.BlockSpec((ts, d), lambda i: (i, 0)),
                  full(1, d), full(*pw.shape), full(*pb.shape), full(1, d), full(1, d),
                  full(2, d, LANES), full(1, LANES)],
        out_specs=[pl.BlockSpec((ts, d), lambda i: (i, 0)),
                   pl.BlockSpec((ts * PACK_ROWS, LANES), lambda i: (i, 0)),
                   pl.BlockSpec((ts, LANES), lambda i: (i, 0))],
        out_shape=[jax.ShapeDtypeStruct((t, d), F32),
                   jax.ShapeDtypeStruct((t * PACK_ROWS, LANES), jnp.uint32),
                   jax.ShapeDtypeStruct((t, LANES), F32)],
        scratch_shapes=[pltpu.VMEM((HALO + ts, d), F32)],
        compiler_params=_cparams("arbitrary"),
        name="pool_layer",
    )(x2d, pn, pw, pb, ps, fn, wr, br)


def _rows8(vals, width):
    row8 = lax.broadcasted_iota(jnp.int32, (8, width), 0)
    out = jnp.zeros((8, width), F32)
    for c, v in enumerate(vals):
        out = jnp.where(row8 == c, v, out)
    return out


def _route_kernel(lg_ref, gate_ref, plan_ref, blk_ref, st_ref, carry_ref, *, tr, nt):
    i = pl.program_id(0)

    @pl.when(i == 0)
    def _():
        carry_ref[...] = jnp.zeros_like(carry_ref)

    lgt = lg_ref[...].T
    neg = jnp.float32(-jnp.inf)
    row8 = lax.broadcasted_iota(jnp.int32, (8, tr), 0)
    row = lax.broadcasted_iota(jnp.int32, (N_EXPERTS, tr), 0)
    first = lambda hit, idx, n: jnp.min(jnp.where(hit, idx, n), axis=0, keepdims=True)

    gl = jnp.where(row8 < N_GROUPS, lgt[N_EXPERTS:N_EXPERTS + 8], neg)
    gmax = jnp.max(gl, axis=0, keepdims=True)
    gidx = first(gl == gmax, row8, 8)
    gprob = 1.0 / jnp.sum(jnp.exp(gl - gmax), axis=0, keepdims=True)

    el = jnp.where(row // EXPERTS_PER_GROUP == gidx, lgt[0:N_EXPERTS], neg)
    m1 = jnp.max(el, axis=0, keepdims=True)
    e0 = first(el == m1, row, N_EXPERTS)
    el2 = jnp.where(row == e0, neg, el)
    m2 = jnp.max(el2, axis=0, keepdims=True)
    e1 = first(el2 == m2, row, N_EXPERTS)
    r = jnp.exp(m2 - m1)
    g0 = gprob / (1.0 + r)
    g1 = gprob * r / (1.0 + r)

    oh0 = jnp.where(row == e0, 1.0, 0.0)
    oh1 = jnp.where(row == e1, 1.0, 0.0)
    oh = oh0 + oh1
    rr = lax.broadcasted_iota(jnp.int32, (tr, tr), 0)
    cc = lax.broadcasted_iota(jnp.int32, (tr, tr), 1)
    earlier = jnp.where(rr < cc, 1.0, 0.0).astype(BF16)
    carry = carry_ref[...]
    before = (jnp.dot(oh.astype(BF16), earlier, preferred_element_type=F32)
              + jnp.tile(carry, (1, tr // LANES)))
    rank0 = jnp.sum(before * oh0, axis=0, keepdims=True)
    rank1 = jnp.sum(before * oh1, axis=0, keepdims=True)
    carry_ref[...] = carry + jnp.sum(oh, axis=1, keepdims=True)

    st_ref[:, pl.ds(pl.multiple_of(i * tr, tr), tr)] = _rows8(
        (e0.astype(F32), e1.astype(F32), rank0, rank1), tr)
    gate_ref[...] = jnp.concatenate([_rows8((g0, g1), tr), jnp.zeros((LANES - 8, tr), F32)], axis=0).T

    @pl.when(i == nt - 1)
    def _():
        cnt = carry_ref[...]
        nblk = jnp.floor((cnt + (ROW_BLOCK - 1)) * (1.0 / ROW_BLOCK))
        er = lax.broadcasted_iota(jnp.int32, cnt.shape, 0)
        ec = lax.broadcasted_iota(jnp.int32, cnt.shape, 1)
        nblk_row = jnp.sum(jnp.where(er == ec, nblk, 0.0), axis=0, keepdims=True)
        pstart = jnp.sum(jnp.where(ec < er, nblk_row, 0.0), axis=1, keepdims=True)
        pend = pstart + nblk[:, 0:1]
        bidx = lax.broadcasted_iota(jnp.int32, (N_EXPERTS, blk_ref.shape[1]), 1).astype(F32)
        block_e = jnp.minimum(jnp.sum(jnp.where(pend <= bidx, 1.0, 0.0), axis=0, keepdims=True),
                              N_EXPERTS - 1.0)
        total = jnp.sum(nblk_row, axis=1, keepdims=True)
        lane_row = lambda col: jnp.concatenate(
            [jnp.sum(jnp.where(er == ec, col, 0.0), axis=0, keepdims=True),
             jnp.zeros((1, blk_ref.shape[1] - LANES), F32)], axis=1)
        pad_at = lane_row(pstart * ROW_BLOCK + cnt)
        pad_len = lane_row(nblk * ROW_BLOCK - cnt)
        blk_ref[...] = _rows8((block_e, jnp.broadcast_to(total, block_e.shape), pad_at, pad_len),
                              blk_ref.shape[1])
        rowf = row.astype(F32)

        def dests(j, c):
            sl = pl.ds(pl.multiple_of(j * tr, tr), tr)
            st = st_ref[:, sl]
            base = lambda e: ROW_BLOCK * jnp.sum(jnp.where(rowf == e, pstart, 0.0), axis=0, keepdims=True)
            plan_ref[:, sl] = _rows8((base(st[0:1]) + st[2:3], base(st[1:2]) + st[3:4]), tr)
            return c

        lax.fori_loop(0, nt, dests, 0)


def _route(logits, tr, n_blocks):
    t = logits.shape[0]
    nt = t // tr
    nbp = -(-n_blocks // LANES) * LANES
    return pl.pallas_call(
        functools.partial(_route_kernel, tr=tr, nt=nt),
        grid=(nt,),
        in_specs=[pl.BlockSpec((tr, LANES), lambda i: (i, 0))],
        out_specs=[pl.BlockSpec((tr, LANES), lambda i: (i, 0)),
                   pl.BlockSpec((8, t), lambda i: (0, 0)),
                   pl.BlockSpec((8, nbp), lambda i: (0, 0))],
        out_shape=[jax.ShapeDtypeStruct((t, LANES), F32),
                   jax.ShapeDtypeStruct((8, t), F32),
                   jax.ShapeDtypeStruct((8, nbp), F32)],
        scratch_shapes=[pltpu.VMEM((8, t), F32), pltpu.VMEM((N_EXPERTS, LANES), F32)],
        compiler_params=_cparams("arbitrary"),
        name="route",
    )(logits)


def _row_tile(ref, r):
    return ref.at[pl.ds(pl.multiple_of(r * SUBLANES, SUBLANES), SUBLANES)]


def _dispatch_kernel(pad_at_ref, pad_len_ref, nu_ref, d0_ref, d1_ref, h_ref, xs_ref, zero_ref, sem, zsem,
                     *, ts, n_blocks):
    first_step = pl.program_id(0) == 0

    def zero_fill(wait):
        def go(cp):
            cp.wait() if wait else cp.start()

        def segment(e, c):
            at, n = pad_at_ref[e], pad_len_ref[e]
            for bit in reversed(range(ROW_BLOCK.bit_length() - 1)):
                size = 1 << bit

                @pl.when((n & size) != 0)
                def _():
                    go(pltpu.make_async_copy(_packed_rows(zero_ref, 0, size),
                                             _packed_rows(xs_ref, at + (n & ~(2 * size - 1)), size), zsem))
            return c

        def tail_block(b, c):
            go(pltpu.make_async_copy(zero_ref, _packed_rows(xs_ref, b * ROW_BLOCK, ROW_BLOCK), zsem))
            return c

        lax.fori_loop(0, N_EXPERTS, segment, 0)
        lax.fori_loop(nu_ref[0], n_blocks, tail_block, 0)

    @pl.when(first_step)
    def _():
        zero_ref[...] = jnp.zeros_like(zero_ref)
        zero_fill(wait=False)

    def issue(r, c):
        for k, d_ref in enumerate((d0_ref, d1_ref)):
            pltpu.make_async_copy(_packed_rows(h_ref, r), _packed_rows(xs_ref, d_ref[0, 0, r]),
                                  sem).start(priority=k)
        return c

    lax.fori_loop(0, ts, issue, 0, unroll=ISSUE_UNROLL)
    for k in range(2):
        pltpu.make_async_copy(h_ref, _packed_rows(xs_ref, 0, ts), sem).wait()

    @pl.when(first_step)
    def _():
        zero_fill(wait=True)


def _dest_spec(ts):
    return pl.BlockSpec((1, 1, ts), lambda i, *_: (i, 0, 0), memory_space=pltpu.SMEM)


def _dispatch(hp, d0, d1, pad_at, pad_len, nb_used, n_blocks, ts):
    return pl.pallas_call(
        functools.partial(_dispatch_kernel, ts=ts, n_blocks=n_blocks),
        grid_spec=pltpu.PrefetchScalarGridSpec(
            num_scalar_prefetch=3,
            grid=(hp.shape[0] // (ts * PACK_ROWS),),
            in_specs=[_dest_spec(ts), _dest_spec(ts),
                      pl.BlockSpec((ts * PACK_ROWS, LANES), lambda i, *_: (i, 0))],
            out_specs=pl.BlockSpec(memory_space=pl.ANY),
            scratch_shapes=[pltpu.VMEM((ROW_BLOCK * PACK_ROWS, LANES), jnp.uint32),
                            pltpu.SemaphoreType.DMA(()), pltpu.SemaphoreType.DMA(())]),
        out_shape=jax.ShapeDtypeStruct((n_blocks * ROW_BLOCK * PACK_ROWS, LANES), jnp.uint32),
        compiler_params=_cparams("arbitrary"),
        name="dispatch",
    )(pad_at, pad_len, nb_used, d0, d1, hp)


def _expert_kernel(be_ref, nu_ref, xs_ref, wg_ref, wu_ref, wd_ref, ys_ref, wg_s, wu_s, wd_s):
    b = pl.program_id(0)

    @pl.when(b < nu_ref[0])
    def _():
        prev = be_ref[jnp.maximum(b - 1, 0)]

        @pl.when((b == 0) | (be_ref[b] != prev))
        def _():
            wg_s[...] = wg_ref[0, 0].astype(BF16)
            wu_s[...] = wu_ref[0, 0].astype(BF16)
            wd_s[...] = wd_ref[0, 0].astype(BF16)

        x = _load_packed_rows(xs_ref, ROW_BLOCK)
        mm = lambda a, w: jnp.dot(a, w, preferred_element_type=F32)
        gt = mm(x, wg_s[...])
        up = mm(x, wu_s[...])
        act = (gt * (1.0 / (1.0 + jnp.exp(-gt))) * up).astype(BF16)
        half = wd_s.shape[1] // 2
        for n0 in (0, half):
            _store_row_tiles(ys_ref, mm(act, wd_s[:, n0:n0 + half]), n0 // LANES)

    @pl.when(b >= nu_ref[0])
    def _():
        ys_ref[...] = jnp.zeros_like(ys_ref)


def _experts(xs, block_e, nb_used, w_gate, w_up, w_down, layer):
    n_blocks = xs.shape[0] // (ROW_BLOCK * PACK_ROWS)
    _, _, d, de = w_gate.shape
    blk = lambda b, be, nu: (jnp.minimum(b, nu[0] - 1), 0)
    wsel = lambda b, be, nu: (layer, be[jnp.minimum(b, nu[0] - 1)], 0, 0)
    return pl.pallas_call(
        _expert_kernel,
        grid_spec=pltpu.PrefetchScalarGridSpec(
            num_scalar_prefetch=2,
            grid=(n_blocks,),
            in_specs=[pl.BlockSpec((ROW_BLOCK * PACK_ROWS, LANES), blk),
                      pl.BlockSpec((1, 1, d, de), wsel),
                      pl.BlockSpec((1, 1, d, de), wsel),
                      pl.BlockSpec((1, 1, de, d), wsel)],
            out_specs=pl.BlockSpec((ROW_BLOCK * SUBLANES, LANES), lambda b, be, nu: (b, 0)),
            scratch_shapes=[pltpu.VMEM((d, de), BF16), pltpu.VMEM((d, de), BF16),
                            pltpu.VMEM((de, d), BF16)]),
        out_shape=jax.ShapeDtypeStruct((n_blocks * ROW_BLOCK * SUBLANES, LANES), F32),
        compiler_params=_cparams("arbitrary"),
        name="experts",
    )(block_e, nb_used, xs, w_gate, w_up, w_down)


def _moe_residual(x, dest_refs, next_dest_refs, gate_ref, ys_ref, ybuf, sems, ts):
    i = pl.program_id(0)
    slot_rows = 2 * ts

    def start(refs, slot):
        def issue(r, c):
            for k, d_ref in enumerate(refs):
                pltpu.make_async_copy(_row_tile(ys_ref, d_ref[0, 0, r]),
                                      _row_tile(ybuf, slot * slot_rows + k * ts + r),
                                      sems.at[slot]).start(priority=k)
            return c

        lax.fori_loop(0, ts, issue, 0, unroll=ISSUE_UNROLL)

    slot = i % 2

    @pl.when(i == 0)
    def _():
        start(dest_refs, 0)

    @pl.when(i + 1 < pl.num_programs(0))
    def _():
        start(next_dest_refs, 1 - slot)

    base = pl.multiple_of(slot * slot_rows * SUBLANES, SUBLANES)
    pltpu.make_async_copy(ys_ref.at[pl.ds(0, slot_rows * SUBLANES)],
                          ybuf.at[pl.ds(base, slot_rows * SUBLANES)], sems.at[slot]).wait()
    gate = gate_ref[...]
    return (x + gate[:, 0:1] * _load_row_tiles(ybuf, ts, base)
            + gate[:, 1:2] * _load_row_tiles(ybuf, ts, base + ts * SUBLANES))


def _moe_residual_specs(ts, n_tiles):
    nxt = pl.BlockSpec((1, 1, ts), lambda i, *_: (jnp.minimum(i + 1, n_tiles - 1), 0, 0),
                       memory_space=pltpu.SMEM)
    specs = [_dest_spec(ts), _dest_spec(ts), nxt, nxt,
             pl.BlockSpec((ts, LANES), lambda i: (i, 0)), pl.BlockSpec(memory_space=pl.ANY)]
    scratch = [pltpu.VMEM((2 * 2 * ts * SUBLANES, LANES), F32), pltpu.SemaphoreType.DMA((2,))]
    return specs, scratch


def _combine_kernel(d0_ref, d1_ref, d0n_ref, d1n_ref, gate_ref, ys_ref, x_ref, fin_ref, out_ref, ybuf, sems,
                    *, ts, final):
    out = _moe_residual(x_ref[...], (d0_ref, d1_ref), (d0n_ref, d1n_ref), gate_ref, ys_ref, ybuf, sems, ts)
    if final:
        out = _rms(out, fin_ref[...])
    out_ref[...] = out


def _combine(x2d, moe, fin, ts, final):
    t, d = x2d.shape
    gate, d0, d1, ys = moe
    specs, scratch = _moe_residual_specs(ts, t // ts)
    return pl.pallas_call(
        functools.partial(_combine_kernel, ts=ts, final=final),
        grid=(t // ts,),
        in_specs=specs + [pl.BlockSpec((ts, d), lambda i: (i, 0)), pl.BlockSpec((1, d), lambda i: (0, 0))],
        out_specs=pl.BlockSpec((ts, d), lambda i: (i, 0)),
        out_shape=jax.ShapeDtypeStruct((t, d), F32),
        scratch_shapes=scratch,
        compiler_params=_cparams("arbitrary"),
        name="combine",
    )(d0, d1, d0, d1, gate, ys, x2d, fin)


def _moe(x2d, hp, logits, w_gate, w_up, w_down, layer, ts):
    t = x2d.shape[0]
    n_blocks = (2 * t) // ROW_BLOCK + N_EXPERTS
    gate, plan, blk = _route(logits, min(512, t), n_blocks)
    d0 = plan[0].astype(jnp.int32).reshape(t // ts, 1, ts)
    d1 = plan[1].astype(jnp.int32).reshape(t // ts, 1, ts)
    block_e = blk[0, :n_blocks].astype(jnp.int32)
    nb_used = blk[1, 0:1].astype(jnp.int32)
    pad_at = blk[2, :N_EXPERTS].astype(jnp.int32)
    pad_len = blk[3, :N_EXPERTS].astype(jnp.int32)
    xs = _dispatch(hp, d0, d1, pad_at, pad_len, nb_used, n_blocks, ts)
    ys = _experts(xs, block_e, nb_used, w_gate, w_up, w_down, layer)
    return gate, d0, d1, ys


def _rope_pairs(t, cs):
    p = t * cs
    return p + pltpu.roll(p, QK_ROPE, axis=1)


def _qkv_kernel(d0_ref, d1_ref, d0n_ref, d1n_ref, gate_ref, ys_ref, x_ref, kn_ref, an_ref, wdkv_ref, kvn_ref,
                wuk_ref, wuvt_ref, wqd_ref, qn_ref, wqut_ref, cs_ref, cst_ref,
                xo_ref, k_ref, vt_ref, qt_ref, ybuf, sems):
    ts = x_ref.shape[0]
    x = _moe_residual(x_ref[...], (d0_ref, d1_ref), (d0n_ref, d1n_ref), gate_ref, ys_ref, ybuf, sems, ts)
    xo_ref[...] = x
    xn = x * lax.rsqrt(jnp.mean(x * x, axis=-1, keepdims=True) + EPS)
    cs = cs_ref[...]
    mm = lambda a, w: jnp.dot(a.astype(BF16), w, preferred_element_type=F32)
    lane = lax.broadcasted_iota(jnp.int32, cs.shape, 1)

    ckv = mm(xn * kn_ref[...], wdkv_ref[...])
    c_kv = _rms(ckv[:, :KV_RANK], kvn_ref[...])
    k_rope = jnp.where(lane < QK_ROPE, _rope_pairs(ckv[:, KV_RANK:], cs), 0.0).astype(BF16)
    k_nope = mm(c_kv, wuk_ref[...]).astype(BF16)
    for h in range(N_HEADS):
        k_ref[0, h] = jnp.concatenate([k_nope[:, h * QK_NOPE:(h + 1) * QK_NOPE], k_rope], axis=-1)

    mmt = lambda wt, a: lax.dot_general(wt, a.astype(BF16), (((1,), (1,)), ((), ())),
                                        preferred_element_type=F32)
    vt = mmt(wuvt_ref[...], c_kv).astype(BF16)
    one_row = jnp.where(lax.broadcasted_iota(jnp.int32, (VT_PAD - V_DIM, ts), 0) == 0, 1.0, 0.0).astype(BF16)
    cq = _rms(mm(xn * an_ref[...], wqd_ref[...]), qn_ref[...])
    qt = mmt(wqut_ref[...], cq) * (ATTN_SCALE * LOG2E)
    cst = cst_ref[...]
    nope_w = N_HEADS * QK_NOPE
    zpad = jnp.zeros((QK_PAD - QK_NOPE - QK_ROPE, ts), F32)
    for h in range(N_HEADS):
        vt_ref[0, h] = jnp.concatenate([vt[h * V_DIM:(h + 1) * V_DIM], one_row], axis=0)
        rp = qt[nope_w + h * LANES:nope_w + (h + 1) * LANES] * cst
        rope = rp[:QK_ROPE] + rp[QK_ROPE:]
        qt_ref[0, h] = jnp.concatenate([qt[h * QK_NOPE:(h + 1) * QK_NOPE], rope, zpad], axis=0).astype(BF16)


def _qkv(x2d, moe, batch, seq, kn, an, wdkv, kvn, wuk, wuvt, wqd, qn, wqut, cs, cst, ts):
    t, d = x2d.shape
    tps = seq // ts
    gate, d0, d1, ys = moe
    specs, scratch = _moe_residual_specs(ts, t // ts)
    full = lambda a: pl.BlockSpec(a.shape, lambda i: (0,) * a.ndim)
    tspec = lambda rows: pl.BlockSpec((1, N_HEADS, rows, ts), lambda i: (i // tps, 0, 0, i % tps))
    return pl.pallas_call(
        _qkv_kernel,
        grid=(t // ts,),
        in_specs=specs + [pl.BlockSpec((ts, d), lambda i: (i, 0)),
                          full(kn), full(an), full(wdkv), full(kvn), full(wuk), full(wuvt), full(wqd), full(qn),
                          full(wqut), pl.BlockSpec((ts, LANES), lambda i: (i % tps, 0)),
                          pl.BlockSpec((LANES, ts), lambda i: (0, i % tps))],
        out_specs=[pl.BlockSpec((ts, d), lambda i: (i, 0)),
                   pl.BlockSpec((1, N_HEADS, ts, QK_PAD), lambda i: (i // tps, 0, i % tps, 0)),
                   tspec(VT_PAD), tspec(QK_PAD)],
        out_shape=[jax.ShapeDtypeStruct((t, d), F32),
                   jax.ShapeDtypeStruct((batch, N_HEADS, seq, QK_PAD), BF16),
                   jax.ShapeDtypeStruct((batch, N_HEADS, VT_PAD, seq), BF16),
                   jax.ShapeDtypeStruct((batch, N_HEADS, QK_PAD, seq), BF16)],
        scratch_shapes=scratch,
        compiler_params=_cparams("arbitrary"),
        name="qkv_proj",
    )(d0, d1, d0, d1, gate, ys, x2d, kn, an, wdkv, kvn, wuk, wuvt, wqd, qn, wqut, cs, cst)


def _attn_kernel(qt_ref, k_ref, vt_ref, o_ref, acc_ref, *, tq):
    i = pl.program_id(2)
    hps = qt_ref.shape[1]
    acc_ref[...] = jnp.zeros_like(acc_ref)

    def tile(j, ms, masked):
        start = pl.multiple_of(j * tq, tq)
        qk = lambda h: jnp.dot(k_ref[0, h, pl.ds(start, tq), :], qt_ref[0, h],
                               preferred_element_type=F32)
        scores = {h: qk(h) for h in range(min(QK_AHEAD, hps))}
        out = []
        for h in range(hps):
            if h + QK_AHEAD < hps:
                scores[h + QK_AHEAD] = qk(h + QK_AHEAD)
            vt = vt_ref[0, h, :, pl.ds(start, tq)]
            s = scores.pop(h)
            if masked:
                kc = lax.broadcasted_iota(jnp.int32, s.shape, 0) // CHUNK
                qc = lax.broadcasted_iota(jnp.int32, s.shape, 1) // CHUNK
                s = jnp.where(kc <= qc, s, NEG_BIG)
            m_new = jnp.maximum(ms[h], jnp.max(s, axis=0, keepdims=True))
            p = jnp.exp2((s - m_new).astype(BF16))
            acc_ref[h] = jnp.exp2(ms[h] - m_new) * acc_ref[h] + jnp.dot(vt, p, preferred_element_type=F32)
            out.append(m_new)
        return tuple(out)

    init = tuple(jnp.full((1, tq), NEG_BIG, F32) for _ in range(hps))
    ms = lax.fori_loop(0, i, lambda j, c: tile(j, c, False), init)
    tile(i, ms, True)
    for h in range(hps):
        acc = acc_ref[h]
        o_ref[0, :, h * V_DIM:(h + 1) * V_DIM] = (acc[:V_DIM] / acc[V_DIM:V_DIM + 1]).T.astype(BF16)


def _attention(qt, kc, vt, tq, hps):
    batch, _, _, seq = qt.shape
    return pl.pallas_call(
        functools.partial(_attn_kernel, tq=tq),
        grid=(batch, N_HEADS // hps, seq // tq),
        in_specs=[pl.BlockSpec((1, hps, QK_PAD, tq), lambda b, h, i: (b, h, 0, i)),
                  pl.BlockSpec((1, hps, seq, QK_PAD), lambda b, h, i: (b, h, 0, 0), pipeline_mode=pl.Buffered(1)),
                  pl.BlockSpec((1, hps, VT_PAD, seq), lambda b, h, i: (b, h, 0, 0), pipeline_mode=pl.Buffered(1))],
        out_specs=pl.BlockSpec((1, tq, hps * V_DIM), lambda b, h, i: (b, i, h)),
        out_shape=jax.ShapeDtypeStruct((batch, seq, N_HEADS * V_DIM), BF16),
        scratch_shapes=[pltpu.VMEM((hps, VT_PAD, tq), F32)],
        compiler_params=_cparams("arbitrary", "arbitrary", "arbitrary"),
        name="attention",
    )(qt, kc, vt)


def _oproj_kernel(o_ref, x_ref, wo_ref, fn_ref, wr_ref, br_ref, x3_ref, hp_ref, lg_ref):
    x3 = x_ref[...] + jnp.dot(o_ref[...], wo_ref[...], preferred_element_type=F32)
    x3_ref[...] = x3
    _ffn_prologue(x3, fn_ref[...], wr_ref[0], wr_ref[1], br_ref[...], hp_ref, lg_ref)


def _oproj(o2d, x2d, wo, fn, wr, br, ts):
    t, d = x2d.shape
    full = lambda a: pl.BlockSpec(a.shape, lambda i: (0,) * a.ndim)
    return pl.pallas_call(
        _oproj_kernel,
        grid=(t // ts,),
        in_specs=[pl.BlockSpec((ts, o2d.shape[1]), lambda i: (i, 0)),
                  pl.BlockSpec((ts, d), lambda i: (i, 0)),
                  full(wo), full(fn), full(wr), full(br)],
        out_specs=[pl.BlockSpec((ts, d), lambda i: (i, 0)),
                   pl.BlockSpec((ts * PACK_ROWS, LANES), lambda i: (i, 0)),
                   pl.BlockSpec((ts, LANES), lambda i: (i, 0))],
        out_shape=[jax.ShapeDtypeStruct((t, d), F32),
                   jax.ShapeDtypeStruct((t * PACK_ROWS, LANES), jnp.uint32),
                   jax.ShapeDtypeStruct((t, LANES), F32)],
        compiler_params=_cparams("arbitrary"),
        name="out_proj",
    )(o2d, x2d, wo, fn, wr, br)


def _router_params(rg_w, rg_b, re_w, re_b):
    d = rg_w.shape[0]
    used = N_GROUPS + N_EXPERTS
    wr = jnp.concatenate([re_w, rg_w, jnp.zeros((d, LANES - used), F32)], axis=1)
    br = jnp.concatenate([re_b, rg_b, jnp.zeros((LANES - used,), F32)])[None, :]
    wr_hi = wr.astype(BF16)
    wr_lo = (wr - wr_hi.astype(F32)).astype(BF16)
    return jnp.stack([wr_hi, wr_lo]), br


def _with_rotate_half(w):
    half = w.shape[-1] // 2
    return jnp.concatenate([w, -w[..., half:], w[..., :half]], axis=-1)


def _rope_table(seq):
    half = QK_ROPE // 2
    inv = ROPE_THETA ** (-jnp.arange(half, dtype=F32) / half)
    ang = jnp.arange(seq, dtype=F32)[:, None] * inv[None, :]
    cos, sin = jnp.cos(ang), jnp.sin(ang)
    return jnp.concatenate([cos, cos, sin, sin], axis=1)


def kernel(x, pool_norm, pool_w, pool_b, pool_scale, kv_in_norm, w_dkv, kv_norm, w_uk, w_uv, attn_norm, wq_down, q_norm, wq_up, wo, ffn_norm, router_group_w, router_group_b, router_expert_w, router_expert_b, w_gate, w_up, w_down, final_norm):
    batch, seq, d = x.shape
    t = batch * seq
    depth = ffn_norm.shape[0]
    n_a = pool_norm.shape[0]
    ts = min(256, seq)
    tq = min(512, seq)
    row = lambda a: a.reshape(1, -1)

    cs = _rope_table(seq)
    routers = [_router_params(router_group_w[l], router_group_b[l], router_expert_w[l], router_expert_b[l])
               for l in range(depth)]
    wdkv = jnp.concatenate([w_dkv[:, :KV_RANK], _with_rotate_half(w_dkv[:, KV_RANK:])], axis=1).astype(BF16)
    wuk, wuvt = w_uk.astype(BF16), w_uv.T.astype(BF16)
    cst = cs.T

    assert n_a >= 1 and d == SUBLANES * LANES
    x2d = x.reshape(t, d)
    kc = vv = None
    moe = None
    for l in range(depth):
        wr, br = routers[l]
        if l < n_a:
            if moe is not None:
                x2d = _combine(x2d, moe, row(final_norm), ts, False)
            x2d, hp, logits = _pool_layer(x2d, seq, row(pool_norm[l]), pool_w[l].astype(BF16), pool_b[l][:, None, :],
                                          row(pool_scale[l]), row(ffn_norm[l]), wr, br, ts)
        else:
            j = l - n_a
            wqu = wq_up[j].reshape(-1, N_HEADS, QK_NOPE + QK_ROPE)
            wqu = jnp.concatenate([wqu[:, :, :QK_NOPE].reshape(-1, N_HEADS * QK_NOPE),
                                   _with_rotate_half(wqu[:, :, QK_NOPE:]).reshape(-1, N_HEADS * LANES)],
                                  axis=1).T.astype(BF16)
            x2d, k_new, v_new, qt = _qkv(x2d, moe, batch, seq, row(kv_in_norm), row(attn_norm[j]), wdkv,
                                         row(kv_norm), wuk, wuvt, wq_down[j].astype(BF16), row(q_norm[j]), wqu,
                                         cs, cst, ts)
            if kc is None:
                kc, vv = k_new, v_new
            o = _attention(qt, kc, vv, tq, 4)
            x2d, hp, logits = _oproj(o.reshape(t, -1), x2d, wo[j].astype(BF16), row(ffn_norm[l]), wr, br, ts)
        moe = _moe(x2d, hp, logits, w_gate, w_up, w_down, l, ts)
    return _combine(x2d, moe, row(final_norm), ts, True).reshape(batch, seq, d)
```

```python
import functools
import math

import jax
import jax.numpy as jnp
from jax import lax
from jax.experimental import pallas as pl
from jax.experimental.pallas import tpu as pltpu

EPS = 1e-6
CHUNK = 64
POOL_WINDOWS = (2, 4, 8, 16)
N_HEADS = 8
QK_NOPE = 128
QK_ROPE = 64
V_DIM = 128
KV_RANK = 256
ROPE_THETA = 10000.0
ATTN_SCALE = 1.0 / math.sqrt(QK_NOPE + QK_ROPE)
N_GROUPS = 4
EXPERTS_PER_GROUP = 8
N_EXPERTS = N_GROUPS * EXPERTS_PER_GROUP
ROW_BLOCK = 256

LANES = 128
SUBLANES = 8
PACK_ROWS = SUBLANES // 2
HALO = 16
QK_PAD = 256
VT_PAD = V_DIM + 16
LOG2E = math.log2(math.e)
VMEM_LIMIT = 56 * 1024 * 1024
NEG_BIG = -1e30
ISSUE_UNROLL = 8

F32 = jnp.float32
BF16 = jnp.bfloat16


def _cparams(*sem):
    return pltpu.CompilerParams(dimension_semantics=sem, vmem_limit_bytes=VMEM_LIMIT)


def _rms(x, g):
    return x * lax.rsqrt(jnp.mean(x * x, axis=-1, keepdims=True) + EPS) * g


def _store_row_tiles(ref, val, chunk0=0):
    n = val.shape[0]
    for c in range(val.shape[1] // LANES):
        ref[pl.ds(chunk0 + c, n, stride=SUBLANES), :] = val[:, c * LANES:(c + 1) * LANES]


def _load_row_tiles(ref, n, base=0):
    return jnp.concatenate([ref[pl.ds(base + c, n, stride=SUBLANES), :] for c in range(SUBLANES)], axis=1)


def _store_packed_rows(ref, val):
    n = val.shape[0]
    bits = pltpu.bitcast(val.astype(F32), jnp.uint32)
    for c in range(PACK_ROWS):
        lo = bits[:, c * LANES:(c + 1) * LANES] >> 16
        hi = bits[:, (c + PACK_ROWS) * LANES:(c + PACK_ROWS + 1) * LANES]
        ref[pl.ds(c, n, stride=PACK_ROWS), :] = lo | hi


def _load_packed_rows(ref, n):
    words = [ref[pl.ds(c, n, stride=PACK_ROWS), :] for c in range(PACK_ROWS)]
    lo = [pltpu.bitcast(w << 16, F32).astype(BF16) for w in words]
    hi = [pltpu.bitcast(w & jnp.uint32(0xFFFF0000), F32).astype(BF16) for w in words]
    return jnp.concatenate(lo + hi, axis=1)


def _packed_rows(ref, r, n=1):
    return ref.at[pl.ds(pl.multiple_of(r * PACK_ROWS, PACK_ROWS), n * PACK_ROWS)]


def _split_bf16(a):
    hi = a.astype(BF16)
    return hi, (a - hi.astype(F32)).astype(BF16)


def _ffn_prologue(x, fn, wr_hi, wr_lo, br, hp_ref, lg_ref):
    h = _rms(x, fn)
    h_hi, h_lo = _split_bf16(h)
    _store_packed_rows(hp_ref, h_hi)
    mm = lambda a, w: jnp.dot(a, w, preferred_element_type=F32)
    lg_ref[...] = mm(h_hi, wr_hi) + mm(h_lo, wr_hi) + mm(h_hi, wr_lo) + br


def _pool_kernel(x_ref, pn_ref, pw_ref, pb_ref, ps_ref, fn_ref, wr_ref, br_ref,
                 x1_ref, hp_ref, lg_ref, buf_ref, *, ts, tiles_per_seq):
    i = pl.program_id(0)
    seq_tile = i % tiles_per_seq
    x = x_ref[...]
    h = _rms(x, pn_ref[...])

    @pl.when(seq_tile == 0)
    def _():
        buf_ref[0:HALO, :] = jnp.zeros((HALO, x.shape[1]), F32)

    buf_ref[HALO:, :] = h
    pos = lax.broadcasted_iota(jnp.int32, (ts, 1), 0) + seq_tile * ts
    gd = x.shape[1] // len(POOL_WINDOWS)
    for g, win in enumerate(POOL_WINDOWS):
        cols = slice(g * gd, (g + 1) * gd)
        s = buf_ref[:, cols]
        k = 1
        while k < win:
            s = s + pltpu.roll(s, k, axis=0)
            k *= 2
        cnt = jnp.minimum(pos + 1, win).astype(F32)
        pooled = s[HALO:, :] / cnt - h[:, cols]
        y = jnp.dot(pooled.astype(BF16), pw_ref[g], preferred_element_type=F32) + pb_ref[g]
        x1_ref[:, cols] = x[:, cols] + y * ps_ref[:, cols]
    buf_ref[0:HALO, :] = h[ts - HALO:, :]
    _ffn_prologue(x1_ref[...], fn_ref[...], wr_ref[0], wr_ref[1], br_ref[...], hp_ref, lg_ref)


def _pool_layer(x2d, seq, pn, pw, pb, ps, fn, wr, br, ts):
    t, d = x2d.shape
    full = lambda *shape: pl.BlockSpec(shape, lambda i: (0,) * len(shape))
    return pl.pallas_call(
        functools.partial(_pool_kernel, ts=ts, tiles_per_seq=seq // ts),
        grid=(t // ts,),
        in_specs=[pl.BlockSpec((ts, d), lambda i: (i, 0)),
                  full(1, d), full(*pw.shape), full(*pb.shape), full(1, d), full(1, d),
                  full(2, d, LANES), full(1, LANES)],
        out_specs=[pl.BlockSpec((ts, d), lambda i: (i, 0)),
                   pl.BlockSpec((ts * PACK_ROWS, LANES), lambda i: (i, 0)),
                   pl.BlockSpec((ts, LANES), lambda i: (i, 0))],
        out_shape=[jax.ShapeDtypeStruct((t, d), F32),
                   jax.ShapeDtypeStruct((t * PACK_ROWS, LANES), jnp.uint32),
                   jax.ShapeDtypeStruct((t, LANES), F32)],
        scratch_shapes=[pltpu.VMEM((HALO + ts, d), F32)],
        compiler_params=_cparams("arbitrary"),
        name="pool_layer",
    )(x2d, pn, pw, pb, ps, fn, wr, br)


def _rows8(vals, width):
    row8 = lax.broadcasted_iota(jnp.int32, (8, width), 0)
    out = jnp.zeros((8, width), F32)
    for c, v in enumerate(vals):
        out = jnp.where(row8 == c, v, out)
    return out


def _route_kernel(lg_ref, gate_ref, plan_ref, blk_ref, st_ref, carry_ref, *, tr, nt):
    i = pl.program_id(0)

    @pl.when(i == 0)
    def _():
        carry_ref[...] = jnp.zeros_like(carry_ref)

    lgt = lg_ref[...].T
    neg = jnp.float32(-jnp.inf)
    row8 = lax.broadcasted_iota(jnp.int32, (8, tr), 0)
    row = lax.broadcasted_iota(jnp.int32, (N_EXPERTS, tr), 0)
    first = lambda hit, idx, n: jnp.min(jnp.where(hit, idx, n), axis=0, keepdims=True)

    gl = jnp.where(row8 < N_GROUPS, lgt[N_EXPERTS:N_EXPERTS + 8], neg)
    gmax = jnp.max(gl, axis=0, keepdims=True)
    gidx = first(gl == gmax, row8, 8)
    gprob = 1.0 / jnp.sum(jnp.exp(gl - gmax), axis=0, keepdims=True)

    el = jnp.where(row // EXPERTS_PER_GROUP == gidx, lgt[0:N_EXPERTS], neg)
    m1 = jnp.max(el, axis=0, keepdims=True)
    e0 = first(el == m1, row, N_EXPERTS)
    el2 = jnp.where(row == e0, neg, el)
    m2 = jnp.max(el2, axis=0, keepdims=True)
    e1 = first(el2 == m2, row, N_EXPERTS)
    r = jnp.exp(m2 - m1)
    g0 = gprob / (1.0 + r)
    g1 = gprob * r / (1.0 + r)

    oh0 = jnp.where(row == e0, 1.0, 0.0)
    oh1 = jnp.where(row == e1, 1.0, 0.0)
    oh = oh0 + oh1
    rr = lax.broadcasted_iota(jnp.int32, (tr, tr), 0)
    cc = lax.broadcasted_iota(jnp.int32, (tr, tr), 1)
    earlier = jnp.where(rr < cc, 1.0, 0.0).astype(BF16)
    carry = carry_ref[...]
    before = (jnp.dot(oh.astype(BF16), earlier, preferred_element_type=F32)
              + jnp.tile(carry, (1, tr // LANES)))
    rank0 = jnp.sum(before * oh0, axis=0, keepdims=True)
    rank1 = jnp.sum(before * oh1, axis=0, keepdims=True)
    carry_ref[...] = carry + jnp.sum(oh, axis=1, keepdims=True)

    st_ref[:, pl.ds(pl.multiple_of(i * tr, tr), tr)] = _rows8(
        (e0.astype(F32), e1.astype(F32), rank0, rank1), tr)
    gate_ref[...] = jnp.concatenate([_rows8((g0, g1), tr), jnp.zeros((LANES - 8, tr), F32)], axis=0).T

    @pl.when(i == nt - 1)
    def _():
        cnt = carry_ref[...]
        nblk = jnp.floor((cnt + (ROW_BLOCK - 1)) * (1.0 / ROW_BLOCK))
        er = lax.broadcasted_iota(jnp.int32, cnt.shape, 0)
        ec = lax.broadcasted_iota(jnp.int32, cnt.shape, 1)
        nblk_row = jnp.sum(jnp.where(er == ec, nblk, 0.0), axis=0, keepdims=True)
        pstart = jnp.sum(jnp.where(ec < er, nblk_row, 0.0), axis=1, keepdims=True)
        pend = pstart + nblk[:, 0:1]
        bidx = lax.broadcasted_iota(jnp.int32, (N_EXPERTS, blk_ref.shape[1]), 1).astype(F32)
        block_e = jnp.minimum(jnp.sum(jnp.where(pend <= bidx, 1.0, 0.0), axis=0, keepdims=True),
                              N_EXPERTS - 1.0)
        total = jnp.sum(nblk_row, axis=1, keepdims=True)
        lane_row = lambda col: jnp.concatenate(
            [jnp.sum(jnp.where(er == ec, col, 0.0), axis=0, keepdims=True),
             jnp.zeros((1, blk_ref.shape[1] - LANES), F32)], axis=1)
        pad_at = lane_row(pstart * ROW_BLOCK + cnt)
        pad_len = lane_row(nblk * ROW_BLOCK - cnt)
        blk_ref[...] = _rows8((block_e, jnp.broadcast_to(total, block_e.shape), pad_at, pad_len),
                              blk_ref.shape[1])
        rowf = row.astype(F32)

        def dests(j, c):
            sl = pl.ds(pl.multiple_of(j * tr, tr), tr)
            st = st_ref[:, sl]
            base = lambda e: ROW_BLOCK * jnp.sum(jnp.where(rowf == e, pstart, 0.0), axis=0, keepdims=True)
            plan_ref[:, sl] = _rows8((base(st[0:1]) + st[2:3], base(st[1:2]) + st[3:4]), tr)
            return c

        lax.fori_loop(0, nt, dests, 0)


def _route(logits, tr, n_blocks):
    t = logits.shape[0]
    nt = t // tr
    nbp = -(-n_blocks // LANES) * LANES
    return pl.pallas_call(
        functools.partial(_route_kernel, tr=tr, nt=nt),
        grid=(nt,),
        in_specs=[pl.BlockSpec((tr, LANES), lambda i: (i, 0))],
        out_specs=[pl.BlockSpec((tr, LANES), lambda i: (i, 0)),
                   pl.BlockSpec((8, t), lambda i: (0, 0)),
                   pl.BlockSpec((8, nbp), lambda i: (0, 0))],
        out_shape=[jax.ShapeDtypeStruct((t, LANES), F32),
                   jax.ShapeDtypeStruct((8, t), F32),
                   jax.ShapeDtypeStruct((8, nbp), F32)],
        scratch_shapes=[pltpu.VMEM((8, t), F32), pltpu.VMEM((N_EXPERTS, LANES), F32)],
        compiler_params=_cparams("arbitrary"),
        name="route",
    )(logits)


def _row_tile(ref, r):
    return ref.at[pl.ds(pl.multiple_of(r * SUBLANES, SUBLANES), SUBLANES)]


def _dispatch_kernel(pad_at_ref, pad_len_ref, nu_ref, d0_ref, d1_ref, h_ref, xs_ref, zero_ref, sem, zsem,
                     *, ts, n_blocks):
    first_step = pl.program_id(0) == 0

    def zero_fill(wait):
        def go(cp):
            cp.wait() if wait else cp.start()

        def segment(e, c):
            at, n = pad_at_ref[e], pad_len_ref[e]
            for bit in reversed(range(ROW_BLOCK.bit_length() - 1)):
                size = 1 << bit

                @pl.when((n & size) != 0)
                def _():
                    go(pltpu.make_async_copy(_packed_rows(zero_ref, 0, size),
                                             _packed_rows(xs_ref, at + (n & ~(2 * size - 1)), size), zsem))
            return c

        def tail_block(b, c):
            go(pltpu.make_async_copy(zero_ref, _packed_rows(xs_ref, b * ROW_BLOCK, ROW_BLOCK), zsem))
            return c

        lax.fori_loop(0, N_EXPERTS, segment, 0)
        lax.fori_loop(nu_ref[0], n_blocks, tail_block, 0)

    @pl.when(first_step)
    def _():
        zero_ref[...] = jnp.zeros_like(zero_ref)
        zero_fill(wait=False)

    def issue(r, c):
        for k, d_ref in enumerate((d0_ref, d1_ref)):
            pltpu.make_async_copy(_packed_rows(h_ref, r), _packed_rows(xs_ref, d_ref[0, 0, r]),
                                  sem).start(priority=k)
        return c

    lax.fori_loop(0, ts, issue, 0, unroll=ISSUE_UNROLL)
    for k in range(2):
        pltpu.make_async_copy(h_ref, _packed_rows(xs_ref, 0, ts), sem).wait()

    @pl.when(first_step)
    def _():
        zero_fill(wait=True)


def _dest_spec(ts):
    return pl.BlockSpec((1, 1, ts), lambda i, *_: (i, 0, 0), memory_space=pltpu.SMEM)


def _dispatch(hp, d0, d1, pad_at, pad_len, nb_used, n_blocks, ts):
    return pl.pallas_call(
        functools.partial(_dispatch_kernel, ts=ts, n_blocks=n_blocks),
        grid_spec=pltpu.PrefetchScalarGridSpec(
            num_scalar_prefetch=3,
            grid=(hp.shape[0] // (ts * PACK_ROWS),),
            in_specs=[_dest_spec(ts), _dest_spec(ts),
                      pl.BlockSpec((ts * PACK_ROWS, LANES), lambda i, *_: (i, 0))],
            out_specs=pl.BlockSpec(memory_space=pl.ANY),
            scratch_shapes=[pltpu.VMEM((ROW_BLOCK * PACK_ROWS, LANES), jnp.uint32),
                            pltpu.SemaphoreType.DMA(()), pltpu.SemaphoreType.DMA(())]),
        out_shape=jax.ShapeDtypeStruct((n_blocks * ROW_BLOCK * PACK_ROWS, LANES), jnp.uint32),
        compiler_params=_cparams("arbitrary"),
        name="dispatch",
    )(pad_at, pad_len, nb_used, d0, d1, hp)


def _expert_kernel(be_ref, nu_ref, xs_ref, wg_ref, wu_ref, wd_ref, ys_ref, wg_s, wu_s, wd_s):
    b = pl.program_id(0)

    @pl.when(b < nu_ref[0])
    def _():
        prev = be_ref[jnp.maximum(b - 1, 0)]

        @pl.when((b == 0) | (be_ref[b] != prev))
        def _():
            wg_s[...] = wg_ref[0, 0].astype(BF16)
            wu_s[...] = wu_ref[0, 0].astype(BF16)
            wd_s[...] = wd_ref[0, 0].astype(BF16)

        x = _load_packed_rows(xs_ref, ROW_BLOCK)
        mm = lambda a, w: jnp.dot(a, w, preferred_element_type=F32)
        gt = mm(x, wg_s[...])
        up = mm(x, wu_s[...])
        act = (gt * (1.0 / (1.0 + jnp.exp(-gt))) * up).astype(BF16)
        half = wd_s.shape[1] // 2
        for n0 in (0, half):
            _store_row_tiles(ys_ref, mm(act, wd_s[:, n0:n0 + half]), n0 // LANES)

    @pl.when(b >= nu_ref[0])
    def _():
        ys_ref[...] = jnp.zeros_like(ys_ref)


def _experts(xs, block_e, nb_used, w_gate, w_up, w_down, layer):
    n_blocks = xs.shape[0] // (ROW_BLOCK * PACK_ROWS)
    _, _, d, de = w_gate.shape
    blk = lambda b, be, nu: (jnp.minimum(b, nu[0] - 1), 0)
    wsel = lambda b, be, nu: (layer, be[jnp.minimum(b, nu[0] - 1)], 0, 0)
    return pl.pallas_call(
        _expert_kernel,
        grid_spec=pltpu.PrefetchScalarGridSpec(
            num_scalar_prefetch=2,
            grid=(n_blocks,),
            in_specs=[pl.BlockSpec((ROW_BLOCK * PACK_ROWS, LANES), blk),
                      pl.BlockSpec((1, 1, d, de), wsel),
                      pl.BlockSpec((1, 1, d, de), wsel),
                      pl.BlockSpec((1, 1, de, d), wsel)],
            out_specs=pl.BlockSpec((ROW_BLOCK * SUBLANES, LANES), lambda b, be, nu: (b, 0)),
            scratch_shapes=[pltpu.VMEM((d, de), BF16), pltpu.VMEM((d, de), BF16),
                            pltpu.VMEM((de, d), BF16)]),
        out_shape=jax.ShapeDtypeStruct((n_blocks * ROW_BLOCK * SUBLANES, LANES), F32),
        compiler_params=_cparams("arbitrary"),
        name="experts",
    )(block_e, nb_used, xs, w_gate, w_up, w_down)


def _moe_residual(x, dest_refs, next_dest_refs, gate_ref, ys_ref, ybuf, sems, ts):
    i = pl.program_id(0)
    slot_rows = 2 * ts

    def start(refs, slot):
        def issue(r, c):
            for k, d_ref in enumerate(refs):
                pltpu.make_async_copy(_row_tile(ys_ref, d_ref[0, 0, r]),
                                      _row_tile(ybuf, slot * slot_rows + k * ts + r),
                                      sems.at[slot]).start(priority=k)
            return c

        lax.fori_loop(0, ts, issue, 0, unroll=ISSUE_UNROLL)

    slot = i % 2

    @pl.when(i == 0)
    def _():
        start(dest_refs, 0)

    @pl.when(i + 1 < pl.num_programs(0))
    def _():
        start(next_dest_refs, 1 - slot)

    base = pl.multiple_of(slot * slot_rows * SUBLANES, SUBLANES)
    pltpu.make_async_copy(ys_ref.at[pl.ds(0, slot_rows * SUBLANES)],
                          ybuf.at[pl.ds(base, slot_rows * SUBLANES)], sems.at[slot]).wait()
    gate = gate_ref[...]
    return (x + gate[:, 0:1] * _load_row_tiles(ybuf, ts, base)
            + gate[:, 1:2] * _load_row_tiles(ybuf, ts, base + ts * SUBLANES))


def _moe_residual_specs(ts, n_tiles):
    nxt = pl.BlockSpec((1, 1, ts), lambda i, *_: (jnp.minimum(i + 1, n_tiles - 1), 0, 0),
                       memory_space=pltpu.SMEM)
    specs = [_dest_spec(ts), _dest_spec(ts), nxt, nxt,
             pl.BlockSpec((ts, LANES), lambda i: (i, 0)), pl.BlockSpec(memory_space=pl.ANY)]
    scratch = [pltpu.VMEM((2 * 2 * ts * SUBLANES, LANES), F32), pltpu.SemaphoreType.DMA((2,))]
    return specs, scratch


def _combine_kernel(d0_ref, d1_ref, d0n_ref, d1n_ref, gate_ref, ys_ref, x_ref, fin_ref, out_ref, ybuf, sems,
                    *, ts, final):
    out = _moe_residual(x_ref[...], (d0_ref, d1_ref), (d0n_ref, d1n_ref), gate_ref, ys_ref, ybuf, sems, ts)
    if final:
        out = _rms(out, fin_ref[...])
    out_ref[...] = out


def _combine(x2d, moe, fin, ts, final):
    t, d = x2d.shape
    gate, d0, d1, ys = moe
    specs, scratch = _moe_residual_specs(ts, t // ts)
    return pl.pallas_call(
        functools.partial(_combine_kernel, ts=ts, final=final),
        grid=(t // ts,),
        in_specs=specs + [pl.BlockSpec((ts, d), lambda i: (i, 0)), pl.BlockSpec((1, d), lambda i: (0, 0))],
        out_specs=pl.BlockSpec((ts, d), lambda i: (i, 0)),
        out_shape=jax.ShapeDtypeStruct((t, d), F32),
        scratch_shapes=scratch,
        compiler_params=_cparams("arbitrary"),
        name="combine",
    )(d0, d1, d0, d1, gate, ys, x2d, fin)


def _moe(x2d, hp, logits, w_gate, w_up, w_down, layer, ts):
    t = x2d.shape[0]
    n_blocks = (2 * t) // ROW_BLOCK + N_EXPERTS
    gate, plan, blk = _route(logits, min(512, t), n_blocks)
    d0 = plan[0].astype(jnp.int32).reshape(t // ts, 1, ts)
    d1 = plan[1].astype(jnp.int32).reshape(t // ts, 1, ts)
    block_e = blk[0, :n_blocks].astype(jnp.int32)
    nb_used = blk[1, 0:1].astype(jnp.int32)
    pad_at = blk[2, :N_EXPERTS].astype(jnp.int32)
    pad_len = blk[3, :N_EXPERTS].astype(jnp.int32)
    xs = _dispatch(hp, d0, d1, pad_at, pad_len, nb_used, n_blocks, ts)
    ys = _experts(xs, block_e, nb_used, w_gate, w_up, w_down, layer)
    return gate, d0, d1, ys


def _rope_pairs(t, cs):
    p = t * cs
    return p + pltpu.roll(p, QK_ROPE, axis=1)


def _qkv_kernel(d0_ref, d1_ref, d0n_ref, d1n_ref, gate_ref, ys_ref, x_ref, kn_ref, an_ref, wdkv_ref, kvn_ref,
                wuk_ref, wuvt_ref, wqd_ref, qn_ref, wqut_ref, cs_ref, cst_ref,
                xo_ref, k_ref, vt_ref, qt_ref, ybuf, sems):
    ts = x_ref.shape[0]
    x = _moe_residual(x_ref[...], (d0_ref, d1_ref), (d0n_ref, d1n_ref), gate_ref, ys_ref, ybuf, sems, ts)
    xo_ref[...] = x
    xn = x * lax.rsqrt(jnp.mean(x * x, axis=-1, keepdims=True) + EPS)
    cs = cs_ref[...]
    mm = lambda a, w: jnp.dot(a.astype(BF16), w, preferred_element_type=F32)
    lane = lax.broadcasted_iota(jnp.int32, cs.shape, 1)

    ckv = mm(xn * kn_ref[...], wdkv_ref[...])
    c_kv = _rms(ckv[:, :KV_RANK], kvn_ref[...])
    k_rope = jnp.where(lane < QK_ROPE, _rope_pairs(ckv[:, KV_RANK:], cs), 0.0).astype(BF16)
    k_nope = mm(c_kv, wuk_ref[...]).astype(BF16)
    for h in range(N_HEADS):
        k_ref[0, h] = jnp.concatenate([k_nope[:, h * QK_NOPE:(h + 1) * QK_NOPE], k_rope], axis=-1)

    mmt = lambda wt, a: lax.dot_general(wt, a.astype(BF16), (((1,), (1,)), ((), ())),
                                        preferred_element_type=F32)
    vt = mmt(wuvt_ref[...], c_kv).astype(BF16)
    one_row = jnp.where(lax.broadcasted_iota(jnp.int32, (VT_PAD - V_DIM, ts), 0) == 0, 1.0, 0.0).astype(BF16)
    cq = _rms(mm(xn * an_ref[...], wqd_ref[...]), qn_ref[...])
    qt = mmt(wqut_ref[...], cq) * (ATTN_SCALE * LOG2E)
    cst = cst_ref[...]
    nope_w = N_HEADS * QK_NOPE
    zpad = jnp.zeros((QK_PAD - QK_NOPE - QK_ROPE, ts), F32)
    for h in range(N_HEADS):
        vt_ref[0, h] = jnp.concatenate([vt[h * V_DIM:(h + 1) * V_DIM], one_row], axis=0)
        rp = qt[nope_w + h * LANES:nope_w + (h + 1) * LANES] * cst
        rope = rp[:QK_ROPE] + rp[QK_ROPE:]
        qt_ref[0, h] = jnp.concatenate([qt[h * QK_NOPE:(h + 1) * QK_NOPE], rope, zpad], axis=0).astype(BF16)


def _qkv(x2d, moe, batch, seq, kn, an, wdkv, kvn, wuk, wuvt, wqd, qn, wqut, cs, cst, ts):
    t, d = x2d.shape
    tps = seq // ts
    gate, d0, d1, ys = moe
    specs, scratch = _moe_residual_specs(ts, t // ts)
    full = lambda a: pl.BlockSpec(a.shape, lambda i: (0,) * a.ndim)
    tspec = lambda rows: pl.BlockSpec((1, N_HEADS, rows, ts), lambda i: (i // tps, 0, 0, i % tps))
    return pl.pallas_call(
        _qkv_kernel,
        grid=(t // ts,),
        in_specs=specs + [pl.BlockSpec((ts, d), lambda i: (i, 0)),
                          full(kn), full(an), full(wdkv), full(kvn), full(wuk), full(wuvt), full(wqd), full(qn),
                          full(wqut), pl.BlockSpec((ts, LANES), lambda i: (i % tps, 0)),
                          pl.BlockSpec((LANES, ts), lambda i: (0, i % tps))],
        out_specs=[pl.BlockSpec((ts, d), lambda i: (i, 0)),
                   pl.BlockSpec((1, N_HEADS, ts, QK_PAD), lambda i: (i // tps, 0, i % tps, 0)),
                   tspec(VT_PAD), tspec(QK_PAD)],
        out_shape=[jax.ShapeDtypeStruct((t, d), F32),
                   jax.ShapeDtypeStruct((batch, N_HEADS, seq, QK_PAD), BF16),
                   jax.ShapeDtypeStruct((batch, N_HEADS, VT_PAD, seq), BF16),
                   jax.ShapeDtypeStruct((batch, N_HEADS, QK_PAD, seq), BF16)],
        scratch_shapes=scratch,
        compiler_params=_cparams("arbitrary"),
        name="qkv_proj",
    )(d0, d1, d0, d1, gate, ys, x2d, kn, an, wdkv, kvn, wuk, wuvt, wqd, qn, wqut, cs, cst)


def _attn_kernel(qt_ref, k_ref, vt_ref, o_ref, s0_ref, s1_ref, cm_ref, m_ref, acc_ref, *, tq):
    i = pl.program_id(2)
    hps = qt_ref.shape[1]
    sbuf = (s0_ref, s1_ref)
    acc_ref[...] = jnp.zeros_like(acc_ref)
    m_ref[...] = jnp.full_like(m_ref, NEG_BIG)

    def scores(j, b, masked=False):
        start = pl.multiple_of(j * tq, tq)
        for h in range(hps):
            s = jnp.dot(k_ref[0, h, pl.ds(start, tq), :], qt_ref[0, h],
                        preferred_element_type=F32)
            if masked:
                kc = lax.broadcasted_iota(jnp.int32, s.shape, 0) // CHUNK
                qc = lax.broadcasted_iota(jnp.int32, s.shape, 1) // CHUNK
                s = jnp.where(kc <= qc, s, NEG_BIG)
            sbuf[b][h] = s
            cm_ref[b, h] = jnp.max(s, axis=0, keepdims=True)

    def values(j, b):
        start = pl.multiple_of(j * tq, tq)
        for h in range(hps):
            m = m_ref[h]
            m_new = jnp.maximum(m, cm_ref[b, h])
            p = jnp.exp2((sbuf[b][h] - m_new).astype(BF16))
            acc_ref[h] = jnp.exp2(m - m_new) * acc_ref[h] + jnp.dot(
                vt_ref[0, h, :, pl.ds(start, tq)], p, preferred_element_type=F32)
            m_ref[h] = m_new

    @pl.when(i == 0)
    def _():
        scores(0, 0, masked=True)
        values(0, 0)

    @pl.when(i > 0)
    def _():
        scores(0, 0)

        def pair(p, c):
            j = 2 * p
            scores(j + 1, 1)
            values(j, 0)
            scores(j + 2, 0)
            values(j + 1, 1)
            return c

        lax.fori_loop(0, (i - 1) // 2, pair, 0)

        @pl.when(i % 2 == 1)
        def _():
            scores(i, 1, masked=True)
            values(i - 1, 0)
            values(i, 1)

        @pl.when(i % 2 == 0)
        def _():
            scores(i - 1, 1)
            values(i - 2, 0)
            scores(i, 0, masked=True)
            values(i - 1, 1)
            values(i, 0)

    for h in range(hps):
        acc = acc_ref[h]
        o_ref[0, :, h * V_DIM:(h + 1) * V_DIM] = (acc[:V_DIM] / acc[V_DIM:V_DIM + 1]).T.astype(BF16)


def _attention(qt, kc, vt, tq, hps):
    batch, _, _, seq = qt.shape
    return pl.pallas_call(
        functools.partial(_attn_kernel, tq=tq),
        grid=(batch, N_HEADS // hps, seq // tq),
        in_specs=[pl.BlockSpec((1, hps, QK_PAD, tq), lambda b, h, i: (b, h, 0, i)),
                  pl.BlockSpec((1, hps, seq, QK_PAD), lambda b, h, i: (b, h, 0, 0), pipeline_mode=pl.Buffered(1)),
                  pl.BlockSpec((1, hps, VT_PAD, seq), lambda b, h, i: (b, h, 0, 0), pipeline_mode=pl.Buffered(1))],
        out_specs=pl.BlockSpec((1, tq, hps * V_DIM), lambda b, h, i: (b, i, h)),
        out_shape=jax.ShapeDtypeStruct((batch, seq, N_HEADS * V_DIM), BF16),
        scratch_shapes=[pltpu.VMEM((hps, tq, tq), F32), pltpu.VMEM((hps, tq, tq), F32),
                        pltpu.VMEM((2, hps, 1, tq), F32), pltpu.VMEM((hps, 1, tq), F32),
                        pltpu.VMEM((hps, VT_PAD, tq), F32)],
        compiler_params=_cparams("arbitrary", "arbitrary", "arbitrary"),
        name="attention",
    )(qt, kc, vt)


def _oproj_kernel(o_ref, x_ref, wo_ref, fn_ref, wr_ref, br_ref, x3_ref, hp_ref, lg_ref):
    x3 = x_ref[...] + jnp.dot(o_ref[...], wo_ref[...], preferred_element_type=F32)
    x3_ref[...] = x3
    _ffn_prologue(x3, fn_ref[...], wr_ref[0], wr_ref[1], br_ref[...], hp_ref, lg_ref)


def _oproj(o2d, x2d, wo, fn, wr, br, ts):
    t, d = x2d.shape
    full = lambda a: pl.BlockSpec(a.shape, lambda i: (0,) * a.ndim)
    return pl.pallas_call(
        _oproj_kernel,
        grid=(t // ts,),
        in_specs=[pl.BlockSpec((ts, o2d.shape[1]), lambda i: (i, 0)),
                  pl.BlockSpec((ts, d), lambda i: (i, 0)),
                  full(wo), full(fn), full(wr), full(br)],
        out_specs=[pl.BlockSpec((ts, d), lambda i: (i, 0)),
                   pl.BlockSpec((ts * PACK_ROWS, LANES), lambda i: (i, 0)),
                   pl.BlockSpec((ts, LANES), lambda i: (i, 0))],
        out_shape=[jax.ShapeDtypeStruct((t, d), F32),
                   jax.ShapeDtypeStruct((t * PACK_ROWS, LANES), jnp.uint32),
                   jax.ShapeDtypeStruct((t, LANES), F32)],
        compiler_params=_cparams("arbitrary"),
        name="out_proj",
    )(o2d, x2d, wo, fn, wr, br)


def _router_params(rg_w, rg_b, re_w, re_b):
    d = rg_w.shape[0]
    used = N_GROUPS + N_EXPERTS
    wr = jnp.concatenate([re_w, rg_w, jnp.zeros((d, LANES - used), F32)], axis=1)
    br = jnp.concatenate([re_b, rg_b, jnp.zeros((LANES - used,), F32)])[None, :]
    wr_hi = wr.astype(BF16)
    wr_lo = (wr - wr_hi.astype(F32)).astype(BF16)
    return jnp.stack([wr_hi, wr_lo]), br


def _with_rotate_half(w):
    half = w.shape[-1] // 2
    return jnp.concatenate([w, -w[..., half:], w[..., :half]], axis=-1)


def _rope_table(seq):
    half = QK_ROPE // 2
    inv = ROPE_THETA ** (-jnp.arange(half, dtype=F32) / half)
    ang = jnp.arange(seq, dtype=F32)[:, None] * inv[None, :]
    cos, sin = jnp.cos(ang), jnp.sin(ang)
    return jnp.concatenate([cos, cos, sin, sin], axis=1)


def kernel(x, pool_norm, pool_w, pool_b, pool_scale, kv_in_norm, w_dkv, kv_norm, w_uk, w_uv, attn_norm, wq_down, q_norm, wq_up, wo, ffn_norm, router_group_w, router_group_b, router_expert_w, router_expert_b, w_gate, w_up, w_down, final_norm):
    batch, seq, d = x.shape
    t = batch * seq
    depth = ffn_norm.shape[0]
    n_a = pool_norm.shape[0]
    ts = min(256, seq)
    tq = min(512, seq)
    row = lambda a: a.reshape(1, -1)

    cs = _rope_table(seq)
    routers = [_router_params(router_group_w[l], router_group_b[l], router_expert_w[l], router_expert_b[l])
               for l in range(depth)]
    wdkv = jnp.concatenate([w_dkv[:, :KV_RANK], _with_rotate_half(w_dkv[:, KV_RANK:])], axis=1).astype(BF16)
    wuk, wuvt = w_uk.astype(BF16), w_uv.T.astype(BF16)
    cst = cs.T

    assert n_a >= 1 and d == SUBLANES * LANES
    x2d = x.reshape(t, d)
    kc = vv = None
    moe = None
    for l in range(depth):
        wr, br = routers[l]
        if l < n_a:
            if moe is not None:
                x2d = _combine(x2d, moe, row(final_norm), ts, False)
            x2d, hp, logits = _pool_layer(x2d, seq, row(pool_norm[l]), pool_w[l].astype(BF16), pool_b[l][:, None, :],
                                          row(pool_scale[l]), row(ffn_norm[l]), wr, br, ts)
        else:
            j = l - n_a
            wqu = wq_up[j].reshape(-1, N_HEADS, QK_NOPE + QK_ROPE)
            wqu = jnp.concatenate([wqu[:, :, :QK_NOPE].reshape(-1, N_HEADS * QK_NOPE),
                                   _with_rotate_half(wqu[:, :, QK_NOPE:]).reshape(-1, N_HEADS * LANES)],
                                  axis=1).T.astype(BF16)
            x2d, k_new, v_new, qt = _qkv(x2d, moe, batch, seq, row(kv_in_norm), row(attn_norm[j]), wdkv,
                                         row(kv_norm), wuk, wuvt, wq_down[j].astype(BF16), row(q_norm[j]), wqu,
                                         cs, cst, ts)
            if kc is None:
                kc, vv = k_new, v_new
            o = _attention(qt, kc, vv, tq, 4)
            x2d, hp, logits = _oproj(o.reshape(t, -1), x2d, wo[j].astype(BF16), row(ffn_norm[l]), wr, br, ts)
        moe = _moe(x2d, hp, logits, w_gate, w_up, w_down, l, ts)
    return _combine(x2d, moe, row(final_norm), ts, True).reshape(batch, seq, d)
```
